```python
import math
import jax
import jax.numpy as jnp
from jax import lax
import numpy as np

D_MODEL = 2048
BATCH = 8
SEQ = 2048
DEPTH = 2

N_MIXERS = 2
N_POOL_LAYERS = (DEPTH + 1) // 2
N_MLA_LAYERS = DEPTH // 2

POOL_WINDOWS = (2, 4, 8, 16)
N_POOL_GROUPS = len(POOL_WINDOWS)
POOL_GROUP_DIM = D_MODEL // N_POOL_GROUPS

MLA_HEADS = 16
Q_LORA_RANK = 512
KV_LORA_RANK = 512
QK_NOPE_DIM = 128
QK_ROPE_DIM = 64
V_HEAD_DIM = 128
ROPE_THETA = 10000.0
Q_BLOCK = 128

MEM_TOKENS = 256
XATTN_HEADS = 4
XATTN_HEAD_DIM = 128

FFN_HIDDEN = ((8 * D_MODEL // 3 + 255) // 256) * 256

NORM_EPS = 1e-6
MASK_VALUE = -1e30

kernel_name = "hybrid_pool_mla_memxattn_swiglu"


def rms_norm(x, g):
    xf = x.astype(jnp.float32)
    var = jnp.mean(xf * xf, axis=-1, keepdims=True)
    return (xf * lax.rsqrt(var + NORM_EPS)).astype(x.dtype) * g


def multiscale_pool_mixer(h, w_pool, scale):
    B, S, D = h.shape
    hg = h.reshape(B, S, N_POOL_GROUPS, POOL_GROUP_DIM).astype(jnp.float32)
    cs = jnp.cumsum(hg, axis=1)
    t = jnp.arange(S)
    means = []
    for g, w in enumerate(POOL_WINDOWS):
        c = cs[:, :, g]
        lagged = jnp.pad(c, ((0, 0), (w, 0), (0, 0)))[:, :S]
        cnt = jnp.minimum(t + 1, w).astype(jnp.float32)[None, :, None]
        means.append((c - lagged) / cnt)
    pooled = (jnp.stack(means, axis=2) - hg).astype(h.dtype)
    mixed = jnp.einsum('bsgc,gcd->bsgd', pooled, w_pool)
    return mixed.reshape(B, S, D) * scale


def rope_cos_sin(positions):
    inv_freq = 1.0 / (ROPE_THETA ** (jnp.arange(0, QK_ROPE_DIM, 2, dtype=jnp.float32) / QK_ROPE_DIM))
    ang = positions.astype(jnp.float32)[..., None] * inv_freq
    return jnp.cos(ang), jnp.sin(ang)


def apply_rope(x, cos, sin):
    half = x.shape[-1] // 2
    x1, x2 = x[..., :half], x[..., half:]
    c = cos[:, :, None, :].astype(x.dtype)
    s = sin[:, :, None, :].astype(x.dtype)
    return jnp.concatenate([x1 * c - x2 * s, x2 * c + x1 * s], axis=-1)


def causal_block_attention(q, k, v, scale):
    B, S, H, Dk = q.shape
    Dv = v.shape[-1]
    nb = S // Q_BLOCK
    qb = q.reshape(B, nb, Q_BLOCK, H, Dk).transpose(1, 0, 2, 3, 4)
    k_pos = jnp.arange(S)

    def one_block(args):
        q_blk, i = args
        s = jnp.einsum('bqhd,bkhd->bhqk', q_blk, k).astype(jnp.float32) * scale
        q_pos = i * Q_BLOCK + jnp.arange(Q_BLOCK)
        mask = k_pos[None, :] <= q_pos[:, None]
        s = jnp.where(mask[None, None], s, MASK_VALUE)
        p = jax.nn.softmax(s, axis=-1).astype(v.dtype)
        return jnp.einsum('bhqk,bkhd->bqhd', p, v)

    out = lax.map(one_block, (qb, jnp.arange(nb)))
    return out.transpose(1, 0, 2, 3, 4).reshape(B, S, H, Dv)


def mla_mixer(h, cos, sin, w_in, q_norm_g, w_q_up, kv_norm_g, w_kv_up, w_out):
    B, S, _ = h.shape
    c = h @ w_in
    c_q = c[..., :Q_LORA_RANK]
    c_kv = c[..., Q_LORA_RANK:Q_LORA_RANK + KV_LORA_RANK]
    k_rope = c[..., Q_LORA_RANK + KV_LORA_RANK:][:, :, None, :]
    q = (rms_norm(c_q, q_norm_g) @ w_q_up).reshape(B, S, MLA_HEADS, QK_NOPE_DIM + QK_ROPE_DIM)
    kv = (rms_norm(c_kv, kv_norm_g) @ w_kv_up).reshape(B, S, MLA_HEADS, QK_NOPE_DIM + V_HEAD_DIM)
    q_nope, q_rope = q[..., :QK_NOPE_DIM], q[..., QK_NOPE_DIM:]
    k_nope, v = kv[..., :QK_NOPE_DIM], kv[..., QK_NOPE_DIM:]
    q_rope = apply_rope(q_rope, cos, sin)
    k_rope = jnp.broadcast_to(apply_rope(k_rope, cos, sin), (B, S, MLA_HEADS, QK_ROPE_DIM))
    q_full = jnp.concatenate([q_nope, q_rope], axis=-1)
    k_full = jnp.concatenate([k_nope, k_rope], axis=-1)
    scale = 1.0 / math.sqrt(QK_NOPE_DIM + QK_ROPE_DIM)
    o = causal_block_attention(q_full, k_full, v, scale)
    return o.reshape(B, S, MLA_HEADS * V_HEAD_DIM) @ w_out


def memory_cross_attention(h, mem_n, w_q, w_kv, w_o):
    B, S, _ = h.shape
    M = mem_n.shape[1]
    width = XATTN_HEADS * XATTN_HEAD_DIM
    q = (h @ w_q).reshape(B, S, XATTN_HEADS, XATTN_HEAD_DIM)
    kv = mem_n @ w_kv
    k = kv[..., :width].reshape(B, M, XATTN_HEADS, XATTN_HEAD_DIM)
    v = kv[..., width:].reshape(B, M, XATTN_HEADS, XATTN_HEAD_DIM)
    s = jnp.einsum('bshd,bmhd->bhsm', q, k).astype(jnp.float32) * (1.0 / math.sqrt(XATTN_HEAD_DIM))
    p = jax.nn.softmax(s, axis=-1).astype(v.dtype)
    o = jnp.einsum('bhsm,bmhd->bshd', p, v)
    return o.reshape(B, S, width) @ w_o


def swiglu_ffn(h, w_in, w_out):
    gu = h @ w_in
    gate, up = gu[..., :FFN_HIDDEN], gu[..., FFN_HIDDEN:]
    return (jax.nn.silu(gate) * up) @ w_out


def _w(key, shape, fan_in):
    return jax.random.normal(key, shape, jnp.float32) * (fan_in ** -0.5)


def _gain(key, shape):
    return 1.0 + 0.05 * jax.random.normal(key, shape, jnp.float32)


def setup_inputs(seed: int = 0) -> dict:
    key = jax.random.key(seed)
    ks = jax.random.split(key, 24)
    D = D_MODEL
    xw = XATTN_HEADS * XATTN_HEAD_DIM
    offsets = jax.random.randint(ks[2], (BATCH, 1), 0, 1024, dtype=jnp.int32)
    positions = offsets + jnp.arange(SEQ, dtype=jnp.int32)[None, :]
    return {
        "x": jax.random.normal(ks[0], (BATCH, SEQ, D), jnp.float32),
        "mem": jax.random.normal(ks[1], (BATCH, MEM_TOKENS, D), jnp.float32),
        "positions": positions,
        "norm_mix_g": _gain(ks[3], (DEPTH, D)),
        "norm_xattn_g": _gain(ks[4], (DEPTH, D)),
        "norm_mem_g": _gain(ks[5], (DEPTH, D)),
        "norm_ffn_g": _gain(ks[6], (DEPTH, D)),
        "pool_w": _w(ks[7], (N_POOL_LAYERS, N_POOL_GROUPS, POOL_GROUP_DIM, POOL_GROUP_DIM), POOL_GROUP_DIM),
        "pool_scale": 1.0 + 0.1 * jax.random.normal(ks[8], (N_POOL_LAYERS, D), jnp.float32),
        "mla_w_in": _w(ks[9], (N_MLA_LAYERS, D, Q_LORA_RANK + KV_LORA_RANK + QK_ROPE_DIM), D),
        "mla_q_norm_g": _gain(ks[10], (N_MLA_LAYERS, Q_LORA_RANK)),
        "mla_w_q_up": _w(ks[11], (N_MLA_LAYERS, Q_LORA_RANK, MLA_HEADS * (QK_NOPE_DIM + QK_ROPE_DIM)), Q_LORA_RANK),
        "mla_kv_norm_g": _gain(ks[12], (N_MLA_LAYERS, KV_LORA_RANK)),
        "mla_w_kv_up": _w(ks[13], (N_MLA_LAYERS, KV_LORA_RANK, MLA_HEADS * (QK_NOPE_DIM + V_HEAD_DIM)), KV_LORA_RANK),
        "mla_w_out": _w(ks[14], (N_MLA_LAYERS, MLA_HEADS * V_HEAD_DIM, D), MLA_HEADS * V_HEAD_DIM),
        "xattn_w_q": _w(ks[15], (DEPTH, D, xw), D),
        "xattn_w_kv": _w(ks[16], (DEPTH, D, 2 * xw), D),
        "xattn_w_o": _w(ks[17], (DEPTH, xw, D), xw),
        "ffn_w_in": _w(ks[18], (DEPTH, D, 2 * FFN_HIDDEN), D),
        "ffn_w_out": _w(ks[19], (DEPTH, FFN_HIDDEN, D), FFN_HIDDEN),
        "final_norm_g": _gain(ks[20], (D,)),
    }


def reference(x, mem, positions, norm_mix_g, norm_xattn_g, norm_mem_g, norm_ffn_g,
              pool_w, pool_scale, mla_w_in, mla_q_norm_g, mla_w_q_up, mla_kv_norm_g,
              mla_w_kv_up, mla_w_out, xattn_w_q, xattn_w_kv, xattn_w_o,
              ffn_w_in, ffn_w_out, final_norm_g):
    cos, sin = rope_cos_sin(positions)
    for i in range(DEPTH):
        j = i // N_MIXERS
        h = rms_norm(x, norm_mix_g[i])
        if i % N_MIXERS == 0:
            x = x + multiscale_pool_mixer(h, pool_w[j], pool_scale[j])
        else:
            x = x + mla_mixer(h, cos, sin, mla_w_in[j], mla_q_norm_g[j], mla_w_q_up[j],
                              mla_kv_norm_g[j], mla_w_kv_up[j], mla_w_out[j])
        h = rms_norm(x, norm_xattn_g[i])
        mem_n = rms_norm(mem, norm_mem_g[i])
        x = x + memory_cross_attention(h, mem_n, xattn_w_q[i], xattn_w_kv[i], xattn_w_o[i])
        h = rms_norm(x, norm_ffn_g[i])
        x = x + swiglu_ffn(h, ffn_w_in[i], ffn_w_out[i])
    return rms_norm(x, final_norm_g)
```

```python
import functools
import math

import jax
import jax.numpy as jnp
from jax import lax
from jax.experimental import pallas as pl
from jax.experimental.pallas import tpu as pltpu

F32 = jnp.float32
BF16 = jnp.bfloat16

NORM_EPS = 1e-6
MASK_VALUE = -1e30
ROPE_THETA = 10000.0
POOL_WINDOWS = (2, 4, 8, 16)
QK_NOPE_DIM = 128
QK_ROPE_DIM = 64
V_HEAD_DIM = 128
XATTN_HEADS = 4

LANES = 128
QK_PAD_DIM = 256
POOL_HALO = 32

TS_MIX = 256
TS_PROJ = 512
TQ_ATTN = 256
TM_FFN = 512
TF_FFN = 512

VMEM_LIMIT = 48 * 1024 * 1024


def _rms(x, g):
    var = jnp.mean(x * x, axis=-1, keepdims=True)
    return x * lax.rsqrt(var + NORM_EPS) * g


def _dot(a, b):
    return jnp.dot(a, b, preferred_element_type=F32)


def _dot_nt(a, b):
    return lax.dot_general(a, b, (((1,), (1,)), ((), ())), preferred_element_type=F32)


def _const_spec(shape):
    nd = len(shape)
    return pl.BlockSpec(shape, lambda *_: (0,) * nd)


def _params(sem):
    return pltpu.CompilerParams(dimension_semantics=sem, vmem_limit_bytes=VMEM_LIMIT)


def _memkv_body(mem_ref, g_ref, w_ref, k_ref, v_ref):
    xw = k_ref.shape[-1]
    mn = _rms(mem_ref[0], g_ref[0]).astype(BF16)
    kv = _dot(mn, w_ref[0])
    k_ref[0, 0] = kv[:, :xw].astype(BF16)
    v_ref[0, 0] = kv[:, xw:].astype(BF16)


def _memkv(mem, g, w_kv):
    depth, d, xw2 = w_kv.shape
    b, m, _ = mem.shape
    xw = xw2 // 2
    out = jax.ShapeDtypeStruct((depth, b, m, xw), BF16)
    return pl.pallas_call(
        _memkv_body,
        grid=(depth, b),
        in_specs=[
            pl.BlockSpec((1, m, d), lambda l, i: (i, 0, 0)),
            pl.BlockSpec((1, 1, d), lambda l, i: (l, 0, 0)),
            pl.BlockSpec((1, d, xw2), lambda l, i: (l, 0, 0)),
        ],
        out_specs=[
            pl.BlockSpec((1, 1, m, xw), lambda l, i: (l, i, 0, 0)),
            pl.BlockSpec((1, 1, m, xw), lambda l, i: (l, i, 0, 0)),
        ],
        out_shape=[out, out],
        compiler_params=_params(("arbitrary", "arbitrary")),
        name="memkv",
    )(mem, g.reshape(depth, 1, d), w_kv)


def _xattn_tail(x1, gx_ref, wq_ref, k_ref, v_ref, wo_ref, gf_ref, x2_ref, h3_ref):
    xw = wq_ref.shape[1]
    hd = xw // XATTN_HEADS
    scale = 1.0 / math.sqrt(hd)
    h2 = _rms(x1, gx_ref[...]).astype(BF16)
    q = _dot(h2, wq_ref[...])
    heads = []
    for n in range(XATTN_HEADS):
        sl = slice(n * hd, (n + 1) * hd)
        s = _dot_nt(q[:, sl].astype(BF16), k_ref[0, 0, :, sl]) * scale
        m = jnp.max(s, axis=-1, keepdims=True)
        p = jnp.exp(s - m)
        l = jnp.sum(p, axis=-1, keepdims=True)
        o = _dot(p.astype(BF16), v_ref[0, 0, :, sl]) * (1.0 / l)
        heads.append(o.astype(BF16))
    o = jnp.concatenate(heads, axis=-1)
    x2 = x1 + _dot(o, wo_ref[...])
    x2_ref[...] = x2
    h3_ref[...] = _rms(x2, gf_ref[...]).astype(BF16)


def _pool_xattn_body(x_ref, halo_ref, gm_ref, pw_ref, ps_ref,
                     gx_ref, wq_ref, k_ref, v_ref, wo_ref, gf_ref,
                     x2_ref, h3_ref, buf_a, buf_b, x1_ref):
    ts, d = x_ref.shape
    dg = d // len(POOL_WINDOWS)
    rows = ts + POOL_HALO
    i = pl.program_id(1)
    x = x_ref[...]
    hx = _rms(x, gm_ref[...])
    hh = _rms(halo_ref[...], gm_ref[...]) * (i > 0).astype(F32)
    t = i * ts + lax.broadcasted_iota(jnp.int32, (ts, 1), 0)
    for g, w in enumerate(POOL_WINDOWS):
        sl = slice(g * dg, (g + 1) * dg)
        buf_a[0:POOL_HALO, :] = hh[:, sl]
        buf_a[POOL_HALO:rows, :] = hx[:, sl]
        src, dst, lo, shift = buf_a, buf_b, 0, w // 2
        while shift > 1:
            lo += 8
            dst[lo:rows, :] = src[lo:rows, :] + src[lo - shift:rows - shift, :]
            src, dst, shift = dst, src, shift // 2
        wsum = src[POOL_HALO:rows, :] + src[POOL_HALO - 1:rows - 1, :]
        inv_cnt = 1.0 / jnp.minimum(t + 1, w).astype(F32)
        pooled = (wsum * inv_cnt - hx[:, sl]).astype(BF16)
        x1_ref[:, sl] = x[:, sl] + _dot(pooled, pw_ref[g]) * ps_ref[:, sl]
    _xattn_tail(x1_ref[...], gx_ref, wq_ref, k_ref, v_ref, wo_ref, gf_ref, x2_ref, h3_ref)


def _pool_xattn(x2d, batch, layer, gm, pool_w, pool_scale, gx, wq, k_mem, v_mem, wo, gf):
    n, d = x2d.shape
    seq = n // batch
    ts = TS_MIX
    nt = seq // ts
    hb = ts // POOL_HALO
    groups, dg, _ = pool_w.shape
    _, _, m, xw = k_mem.shape
    row = lambda b, i: (b * nt + i, 0)
    halo = lambda b, i: (jnp.maximum((b * nt + i) * hb - 1, 0), 0)
    memkv = lambda b, i: (layer, b, 0, 0)
    return pl.pallas_call(
        _pool_xattn_body,
        grid=(batch, nt),
        in_specs=[
            pl.BlockSpec((ts, d), row),
            pl.BlockSpec((POOL_HALO, d), halo),
            _const_spec((1, d)),
            _const_spec((groups, dg, dg)),
            _const_spec((1, d)),
            _const_spec((1, d)),
            _const_spec((d, xw)),
            pl.BlockSpec((1, 1, m, xw), memkv),
            pl.BlockSpec((1, 1, m, xw), memkv),
            _const_spec((xw, d)),
            _const_spec((1, d)),
        ],
        out_specs=[pl.BlockSpec((ts, d), row), pl.BlockSpec((ts, d), row)],
        out_shape=[jax.ShapeDtypeStruct((n, d), F32), jax.ShapeDtypeStruct((n, d), BF16)],
        scratch_shapes=[
            pltpu.VMEM((ts + POOL_HALO, dg), F32),
            pltpu.VMEM((ts + POOL_HALO, dg), F32),
            pltpu.VMEM((ts, d), F32),
        ],
        compiler_params=_params(("arbitrary", "arbitrary")),
        name="pool_xattn",
    )(x2d, x2d, gm, pool_w, pool_scale, gx, wq, k_mem, v_mem, wo, gf)


def _oproj_xattn_body(x_ref, o_ref, wout_ref,
                      gx_ref, wq_ref, k_ref, v_ref, wo_ref, gf_ref, x2_ref, h3_ref):
    x1 = x_ref[...] + _dot(o_ref[...], wout_ref[...])
    _xattn_tail(x1, gx_ref, wq_ref, k_ref, v_ref, wo_ref, gf_ref, x2_ref, h3_ref)


def _oproj_xattn(x2d, o2d, batch, layer, w_out, gx, wq, k_mem, v_mem, wo, gf):
    n, d = x2d.shape
    seq = n // batch
    ts = TS_MIX
    nt = seq // ts
    _, _, m, xw = k_mem.shape
    row = lambda b, i: (b * nt + i, 0)
    memkv = lambda b, i: (layer, b, 0, 0)
    return pl.pallas_call(
        _oproj_xattn_body,
        grid=(batch, nt),
        in_specs=[
            pl.BlockSpec((ts, d), row),
            pl.BlockSpec((ts, o2d.shape[1]), row),
            _const_spec(w_out.shape),
            _const_spec((1, d)),
            _const_spec((d, xw)),
            pl.BlockSpec((1, 1, m, xw), memkv),
            pl.BlockSpec((1, 1, m, xw), memkv),
            _const_spec((xw, d)),
            _const_spec((1, d)),
        ],
        out_specs=[pl.BlockSpec((ts, d), row), pl.BlockSpec((ts, d), row)],
        out_shape=[jax.ShapeDtypeStruct((n, d), F32), jax.ShapeDtypeStruct((n, d), BF16)],
        compiler_params=_params(("arbitrary", "arbitrary")),
        name="oproj_xattn",
    )(x2d, o2d, w_out, gx, wq, k_mem, v_mem, wo, gf)


def _ffn_body(final, h_ref, x_ref, wg_ref, wu_ref, wo_ref, g_ref, acc_ref, *hn_ref):
    k = pl.program_id(1)

    @pl.when(k == 0)
    def _():
        acc_ref[...] = x_ref[...]

    h = h_ref[...]
    gate = _dot(h, wg_ref[...])
    up = _dot(h, wu_ref[...])
    act = (gate / (1.0 + jnp.exp(-gate)) * up).astype(BF16)
    acc_ref[...] += _dot(act, wo_ref[...])

    @pl.when(k == pl.num_programs(1) - 1)
    def _():
        normed = _rms(acc_ref[...], g_ref[...])
        if final:
            acc_ref[...] = normed
        else:
            hn_ref[0][...] = normed.astype(BF16)


def _ffn(h2d, x2d, w_in, w_out, g_next, final):
    n, d = x2d.shape
    f = w_out.shape[0]
    tm, tf = TM_FFN, TF_FFN
    nf = f // tf
    row = lambda i, k: (i, 0)
    out_shape = [jax.ShapeDtypeStruct((n, d), F32)]
    out_specs = [pl.BlockSpec((tm, d), row)]
    if not final:
        out_shape.append(jax.ShapeDtypeStruct((n, d), BF16))
        out_specs.append(pl.BlockSpec((tm, d), row))
    return pl.pallas_call(
        functools.partial(_ffn_body, final),
        grid=(n // tm, nf),
        in_specs=[
            pl.BlockSpec((tm, d), row),
            pl.BlockSpec((tm, d), row),
            pl.BlockSpec((d, tf), lambda i, k: (0, k)),
            pl.BlockSpec((d, tf), lambda i, k: (0, nf + k)),
            pl.BlockSpec((tf, d), lambda i, k: (k, 0)),
            _const_spec((1, d)),
        ],
        out_specs=out_specs,
        out_shape=out_shape,
        compiler_params=_params(("arbitrary", "arbitrary")),
        name="ffn_final" if final else "ffn",
    )(h2d, x2d, w_in, w_in, w_out, g_next)


def _mla_proj_body(h_ref, pos_ref, win_ref, gq_ref, gkv_ref, wq_ref, wk_ref, wv_ref,
                   freq_ref, sign_ref, q_ref, k_ref, v_ref):
    ts = h_ref.shape[0]
    rq = gq_ref.shape[1]
    rkv = gkv_ref.shape[1]
    heads = v_ref.shape[1] // V_HEAD_DIM
    half = QK_ROPE_DIM // 2
    c = _dot(h_ref[...], win_ref[...])
    cq = _rms(c[:, :rq], gq_ref[...]).astype(BF16)
    ckv = _rms(c[:, rq:rq + rkv], gkv_ref[...]).astype(BF16)

    ang = pos_ref[...] * freq_ref[...]
    cos = jnp.cos(ang)
    sin_signed = jnp.sin(ang) * sign_ref[...]
    lane = lax.broadcasted_iota(jnp.int32, (ts, LANES), 1)

    def rope(x):
        swapped = jnp.where(lane < half,
                            pltpu.roll(x, LANES - half, axis=1),
                            pltpu.roll(x, half, axis=1))
        return x * cos + swapped * sin_signed

    k_rope = rope(c[:, rq + rkv:]).astype(BF16)
    group = 4
    for hc in range(heads // group):
        qc = _dot(cq, wq_ref[:, hc * group * QK_PAD_DIM:(hc + 1) * group * QK_PAD_DIM])
        for j in range(group):
            dst = (hc * group + j) * QK_PAD_DIM
            src = j * QK_PAD_DIM
            q_ref[:, dst:dst + QK_NOPE_DIM] = qc[:, src:src + QK_NOPE_DIM].astype(BF16)
            q_ref[:, dst + QK_NOPE_DIM:dst + QK_PAD_DIM] = rope(
                qc[:, src + QK_NOPE_DIM:src + QK_PAD_DIM]).astype(BF16)
    kn = _dot(ckv, wk_ref[...])
    for n in range(heads):
        dst = n * QK_PAD_DIM
        k_ref[:, dst:dst + QK_NOPE_DIM] = kn[:, n * QK_NOPE_DIM:(n + 1) * QK_NOPE_DIM].astype(BF16)
        k_ref[:, dst + QK_NOPE_DIM:dst + QK_PAD_DIM] = k_rope
    v_ref[...] = _dot(ckv, wv_ref[...]).astype(BF16)


def _mla_proj(h2d, pos, w_in, gq, gkv, wq, wk, wv, freq, sign):
    n, d = h2d.shape
    ts = TS_PROJ
    row = lambda i: (i, 0)
    return pl.pallas_call(
        _mla_proj_body,
        grid=(n // ts,),
        in_specs=[
            pl.BlockSpec((ts, d), row),
            pl.BlockSpec((ts, 1), row),
            _const_spec(w_in.shape),
            _const_spec(gq.shape),
            _const_spec(gkv.shape),
            _const_spec(wq.shape),
            _const_spec(wk.shape),
            _const_spec(wv.shape),
            _const_spec(freq.shape),
            _const_spec(sign.shape),
        ],
        out_specs=[
            pl.BlockSpec((ts, wq.shape[1]), row),
            pl.BlockSpec((ts, wq.shape[1]), row),
            pl.BlockSpec((ts, wv.shape[1]), row),
        ],
        out_shape=[
            jax.ShapeDtypeStruct((n, wq.shape[1]), BF16),
            jax.ShapeDtypeStruct((n, wq.shape[1]), BF16),
            jax.ShapeDtypeStruct((n, wv.shape[1]), BF16),
        ],
        compiler_params=_params(("arbitrary",)),
        name="mla_proj",
    )(h2d, pos, w_in, gq, gkv, wq, wk, wv, freq, sign)


def _attn_body(scale, q_ref, k_ref, v_ref, o_ref):
    seq = q_ref.shape[0]
    tq = TQ_ATTN
    row = lax.broadcasted_iota(jnp.int32, (tq, tq), 0)
    col = lax.broadcasted_iota(jnp.int32, (tq, tq), 1)
    causal = col <= row
    for qi in range(seq // tq):
        lo, hi = qi * tq, (qi + 1) * tq
        q = q_ref[lo:hi, :]
        s_diag = jnp.where(causal, _dot_nt(q, k_ref[lo:hi, :]) * scale, MASK_VALUE)
        m = jnp.max(s_diag, axis=-1, keepdims=True)
        if qi > 0:
            s_past = _dot_nt(q, k_ref[0:lo, :]) * scale
            m = jnp.maximum(m, jnp.max(s_past, axis=-1, keepdims=True))
        p_diag = jnp.exp(s_diag - m)
        l = jnp.sum(p_diag, axis=-1, keepdims=True)
        acc = _dot(p_diag.astype(BF16), v_ref[lo:hi, :])
        if qi > 0:
            p_past = jnp.exp(s_past - m)
            l = l + jnp.sum(p_past, axis=-1, keepdims=True)
            acc = acc + _dot(p_past.astype(BF16), v_ref[0:lo, :])
        o_ref[lo:hi, :] = (acc * (1.0 / l)).astype(BF16)


def _attention(q, k, v, batch, heads):
    n = q.shape[0]
    seq = n // batch
    scale = 1.0 / math.sqrt(QK_NOPE_DIM + QK_ROPE_DIM)
    bh = lambda b, h: (b, h)
    return pl.pallas_call(
        functools.partial(_attn_body, scale),
        grid=(batch, heads),
        in_specs=[
            pl.BlockSpec((seq, QK_PAD_DIM), bh),
            pl.BlockSpec((seq, QK_PAD_DIM), bh),
            pl.BlockSpec((seq, V_HEAD_DIM), bh),
        ],
        out_specs=pl.BlockSpec((seq, V_HEAD_DIM), bh),
        out_shape=jax.ShapeDtypeStruct((n, heads * V_HEAD_DIM), BF16),
        compiler_params=_params(("arbitrary", "arbitrary")),
        name="mla_attention",
    )(q, k, v)


def _pad_q_up(w):
    r = w.shape[0]
    heads = w.shape[1] // (QK_NOPE_DIM + QK_ROPE_DIM)
    w = w.reshape(r, heads, QK_NOPE_DIM + QK_ROPE_DIM)
    pad = jnp.zeros((r, heads, QK_PAD_DIM - QK_NOPE_DIM - QK_ROPE_DIM), w.dtype)
    return jnp.concatenate([w, pad], axis=-1).reshape(r, heads * QK_PAD_DIM).astype(BF16)


def _split_kv_up(w):
    r = w.shape[0]
    heads = w.shape[1] // (QK_NOPE_DIM + V_HEAD_DIM)
    w = w.reshape(r, heads, QK_NOPE_DIM + V_HEAD_DIM)
    wk = w[:, :, :QK_NOPE_DIM].reshape(r, heads * QK_NOPE_DIM)
    wv = w[:, :, QK_NOPE_DIM:].reshape(r, heads * V_HEAD_DIM)
    return wk.astype(BF16), wv.astype(BF16)


def _rope_rows():
    half = QK_ROPE_DIM // 2
    inv_freq = 1.0 / (ROPE_THETA ** (jnp.arange(0, QK_ROPE_DIM, 2, dtype=F32) / QK_ROPE_DIM))
    zeros = jnp.zeros((LANES - QK_ROPE_DIM,), F32)
    freq = jnp.concatenate([inv_freq, inv_freq, zeros]).reshape(1, LANES)
    sign = jnp.concatenate([-jnp.ones((half,), F32), jnp.ones((half,), F32), zeros]).reshape(1, LANES)
    return freq, sign


def kernel(x, mem, positions, norm_mix_g, norm_xattn_g, norm_mem_g, norm_ffn_g, pool_w, pool_scale, mla_w_in, mla_q_norm_g, mla_w_q_up, mla_kv_norm_g, mla_w_kv_up, mla_w_out, xattn_w_q, xattn_w_kv, xattn_w_o, ffn_w_in, ffn_w_out, final_norm_g):
    batch, seq, d = x.shape
    depth = norm_mix_g.shape[0]
    n_mixers = 2
    row = lambda g: g.reshape(1, -1)

    k_mem, v_mem = _memkv(mem, norm_mem_g, xattn_w_kv.astype(BF16))
    freq, sign = _rope_rows()
    pos = positions.reshape(batch * seq, 1).astype(F32)

    x2d = x.reshape(batch * seq, d)
    h_mix = None
    for i in range(depth):
        j = i // n_mixers
        xattn_args = (row(norm_xattn_g[i]), xattn_w_q[i].astype(BF16), k_mem, v_mem,
                      xattn_w_o[i].astype(BF16), row(norm_ffn_g[i]))
        if i % n_mixers == 0:
            x2d, h_ffn = _pool_xattn(x2d, batch, i, row(norm_mix_g[i]), pool_w[j].astype(BF16),
                                     row(pool_scale[j]), *xattn_args)
        else:
            rq = mla_q_norm_g.shape[1]
            rkv = mla_kv_norm_g.shape[1]
            w_in = mla_w_in[j]
            w_in = jnp.pad(w_in, ((0, 0), (0, rq + rkv + LANES - w_in.shape[1]))).astype(BF16)
            wq = _pad_q_up(mla_w_q_up[j])
            wk, wv = _split_kv_up(mla_w_kv_up[j])
            heads = wv.shape[1] // V_HEAD_DIM
            q, k, v = _mla_proj(h_mix, pos, w_in, row(mla_q_norm_g[j]), row(mla_kv_norm_g[j]),
                                wq, wk, wv, freq, sign)
            o = _attention(q, k, v, batch, heads)
            x2d, h_ffn = _oproj_xattn(x2d, o, batch, i, mla_w_out[j].astype(BF16), *xattn_args)
        final = i == depth - 1
        g_next = final_norm_g if final else norm_mix_g[i + 1]
        outs = _ffn(h_ffn, x2d, ffn_w_in[i].astype(BF16), ffn_w_out[i].astype(BF16), row(g_next), final)
        x2d = outs[0]
        if not final:
            h_mix = outs[1]
    return x2d.reshape(batch, seq, d)
```

```python
import functools
import math

import jax
import jax.numpy as jnp
from jax import lax
from jax.experimental import pallas as pl
from jax.experimental.pallas import tpu as pltpu

F32 = jnp.float32
BF16 = jnp.bfloat16

NORM_EPS = 1e-6
MASK_VALUE = -1e30
ROPE_THETA = 10000.0
POOL_WINDOWS = (2, 4, 8, 16)
QK_NOPE_DIM = 128
QK_ROPE_DIM = 64
V_HEAD_DIM = 128
XATTN_HEADS = 4

LANES = 128
QK_PAD_DIM = 256
POOL_HALO = 32

TS_MIX = 256
TS_PROJ = 512
TQ_ATTN = 256
TM_FFN = 1024
TF_FFN = 512

VMEM_LIMIT = 48 * 1024 * 1024
VMEM_LIMIT_FFN = 56 * 1024 * 1024


def _rms(x, g):
    var = jnp.mean(x * x, axis=-1, keepdims=True)
    return x * lax.rsqrt(var + NORM_EPS) * g


def _dot(a, b):
    return jnp.dot(a, b, preferred_element_type=F32)


def _dot_nt(a, b):
    return lax.dot_general(a, b, (((1,), (1,)), ((), ())), preferred_element_type=F32)


def _const_spec(shape):
    nd = len(shape)
    return pl.BlockSpec(shape, lambda *_: (0,) * nd)


def _layer_spec(stacked, layer):
    tail = stacked.shape[1:]
    return pl.BlockSpec((1,) + tail, lambda *_: (layer,) + (0,) * len(tail))


def _params(sem, vmem_limit=VMEM_LIMIT):
    return pltpu.CompilerParams(dimension_semantics=sem, vmem_limit_bytes=vmem_limit)


def _memkv_body(mem_ref, g_ref, w_ref, k_ref, v_ref):
    xw = k_ref.shape[-1]
    mn = _rms(mem_ref[0], g_ref[0]).astype(BF16)
    kv = _dot(mn, w_ref[0])
    k_ref[0, 0] = kv[:, :xw].astype(BF16)
    v_ref[0, 0] = kv[:, xw:].astype(BF16)


def _memkv(mem, g, w_kv):
    depth, d, xw2 = w_kv.shape
    b, m, _ = mem.shape
    xw = xw2 // 2
    out = jax.ShapeDtypeStruct((depth, b, m, xw), BF16)
    return pl.pallas_call(
        _memkv_body,
        grid=(depth, b),
        in_specs=[
            pl.BlockSpec((1, m, d), lambda l, i: (i, 0, 0)),
            pl.BlockSpec((1, 1, d), lambda l, i: (l, 0, 0)),
            pl.BlockSpec((1, d, xw2), lambda l, i: (l, 0, 0)),
        ],
        out_specs=[
            pl.BlockSpec((1, 1, m, xw), lambda l, i: (l, i, 0, 0)),
            pl.BlockSpec((1, 1, m, xw), lambda l, i: (l, i, 0, 0)),
        ],
        out_shape=[out, out],
        compiler_params=_params(("arbitrary", "arbitrary")),
        name="memkv",
    )(mem, g.reshape(depth, 1, d), w_kv)


def _xattn_tail(x1, gx_ref, wq_ref, k_ref, v_ref, wo_ref, gf_ref, x2_ref, h3_ref):
    xw = wq_ref.shape[-1]
    hd = xw // XATTN_HEADS
    scale = 1.0 / math.sqrt(hd)
    h2 = _rms(x1, gx_ref[0]).astype(BF16)
    q = _dot(h2, wq_ref[0])
    heads = []
    for n in range(XATTN_HEADS):
        sl = slice(n * hd, (n + 1) * hd)
        s = _dot_nt(q[:, sl].astype(BF16), k_ref[0, 0, :, sl]) * scale
        m = jnp.max(s, axis=-1, keepdims=True)
        p = jnp.exp(s - m)
        l = jnp.sum(p, axis=-1, keepdims=True)
        o = _dot(p.astype(BF16), v_ref[0, 0, :, sl]) * (1.0 / l)
        heads.append(o.astype(BF16))
    o = jnp.concatenate(heads, axis=-1)
    x2 = x1 + _dot(o, wo_ref[0])
    x2_ref[...] = x2
    h3_ref[...] = _rms(x2, gf_ref[0]).astype(BF16)


def _pool_xattn_body(x_ref, halo_ref, gm_ref, pw_ref, ps_ref,
                     gx_ref, wq_ref, k_ref, v_ref, wo_ref, gf_ref,
                     x2_ref, h3_ref, buf_a, buf_b, x1_ref):
    ts, d = x_ref.shape
    dg = d // len(POOL_WINDOWS)
    rows = ts + POOL_HALO
    i = pl.program_id(1)
    x = x_ref[...]
    hx = _rms(x, gm_ref[0])
    hh = _rms(halo_ref[...], gm_ref[0]) * (i > 0).astype(F32)
    t = i * ts + lax.broadcasted_iota(jnp.int32, (ts, 1), 0)
    for g, w in enumerate(POOL_WINDOWS):
        sl = slice(g * dg, (g + 1) * dg)
        buf_a[0:POOL_HALO, :] = hh[:, sl]
        buf_a[POOL_HALO:rows, :] = hx[:, sl]
        src, dst, lo, shift = buf_a, buf_b, 0, w // 2
        while shift > 1:
            lo += 8
            dst[lo:rows, :] = src[lo:rows, :] + src[lo - shift:rows - shift, :]
            src, dst, shift = dst, src, shift // 2
        wsum = src[POOL_HALO:rows, :] + src[POOL_HALO - 1:rows - 1, :]
        inv_cnt = 1.0 / jnp.minimum(t + 1, w).astype(F32)
        pooled = (wsum * inv_cnt - hx[:, sl]).astype(BF16)
        x1_ref[:, sl] = x[:, sl] + _dot(pooled, pw_ref[0, g]) * ps_ref[0, :, sl]
    _xattn_tail(x1_ref[...], gx_ref, wq_ref, k_ref, v_ref, wo_ref, gf_ref, x2_ref, h3_ref)


def _pool_xattn(x2d, batch, layer, pool_layer, gm, pool_w, pool_scale, gx, wq, k_mem, v_mem, wo, gf):
    n, d = x2d.shape
    seq = n // batch
    ts = TS_MIX
    nt = seq // ts
    hb = ts // POOL_HALO
    dg = pool_w.shape[-1]
    _, _, m, xw = k_mem.shape
    row = lambda b, i: (b * nt + i, 0)
    halo = lambda b, i: (jnp.maximum((b * nt + i) * hb - 1, 0), 0)
    memkv = lambda b, i: (layer, b, 0, 0)
    return pl.pallas_call(
        _pool_xattn_body,
        grid=(batch, nt),
        in_specs=[
            pl.BlockSpec((ts, d), row),
            pl.BlockSpec((POOL_HALO, d), halo),
            _layer_spec(gm, layer),
            _layer_spec(pool_w, pool_layer),
            _layer_spec(pool_scale, pool_layer),
            _layer_spec(gx, layer),
            _layer_spec(wq, layer),
            pl.BlockSpec((1, 1, m, xw), memkv),
            pl.BlockSpec((1, 1, m, xw), memkv),
            _layer_spec(wo, layer),
            _layer_spec(gf, layer),
        ],
        out_specs=[pl.BlockSpec((ts, d), row), pl.BlockSpec((ts, d), row)],
        out_shape=[jax.ShapeDtypeStruct((n, d), F32), jax.ShapeDtypeStruct((n, d), BF16)],
        scratch_shapes=[
            pltpu.VMEM((ts + POOL_HALO, dg), F32),
            pltpu.VMEM((ts + POOL_HALO, dg), F32),
            pltpu.VMEM((ts, d), F32),
        ],
        compiler_params=_params(("arbitrary", "arbitrary")),
        name="pool_xattn",
    )(x2d, x2d, gm, pool_w, pool_scale, gx, wq, k_mem, v_mem, wo, gf)


def _oproj_xattn_body(x_ref, o_ref, wout_ref,
                      gx_ref, wq_ref, k_ref, v_ref, wo_ref, gf_ref, x2_ref, h3_ref):
    x1 = x_ref[...] + _dot(o_ref[...], wout_ref[0])
    _xattn_tail(x1, gx_ref, wq_ref, k_ref, v_ref, wo_ref, gf_ref, x2_ref, h3_ref)


def _oproj_xattn(x2d, o2d, batch, layer, mla_layer, w_out, gx, wq, k_mem, v_mem, wo, gf):
    n, d = x2d.shape
    seq = n // batch
    ts = TS_MIX
    nt = seq // ts
    _, _, m, xw = k_mem.shape
    row = lambda b, i: (b * nt + i, 0)
    memkv = lambda b, i: (layer, b, 0, 0)
    return pl.pallas_call(
        _oproj_xattn_body,
        grid=(batch, nt),
        in_specs=[
            pl.BlockSpec((ts, d), row),
            pl.BlockSpec((ts, o2d.shape[1]), row),
            _layer_spec(w_out, mla_layer),
            _layer_spec(gx, layer),
            _layer_spec(wq, layer),
            pl.BlockSpec((1, 1, m, xw), memkv),
            pl.BlockSpec((1, 1, m, xw), memkv),
            _layer_spec(wo, layer),
            _layer_spec(gf, layer),
        ],
        out_specs=[pl.BlockSpec((ts, d), row), pl.BlockSpec((ts, d), row)],
        out_shape=[jax.ShapeDtypeStruct((n, d), F32), jax.ShapeDtypeStruct((n, d), BF16)],
        compiler_params=_params(("arbitrary", "arbitrary")),
        name="oproj_xattn",
    )(x2d, o2d, w_out, gx, wq, k_mem, v_mem, wo, gf)


def _ffn_body(final, h_ref, x_hbm, wg_ref, wu_ref, wo_ref, g_ref, acc_ref, xbuf, sem):
    i = pl.program_id(0)
    k = pl.program_id(1)
    tm = acc_ref.shape[0]

    def x_copy(tile):
        start = tile * tm
        if not isinstance(start, int):
            start = pl.multiple_of(start, tm)
        return pltpu.make_async_copy(x_hbm.at[pl.ds(start, tm), :], xbuf, sem)

    @pl.when(jnp.logical_and(i == 0, k == 0))
    def _():
        x_copy(0).start()

    @pl.when(k == 0)
    def _():
        x_copy(i).wait()
        acc_ref[...] = xbuf[...]

    @pl.when(jnp.logical_and(k == 1, i + 1 < pl.num_programs(0)))
    def _():
        x_copy(i + 1).start()

    h = h_ref[...]
    gate = _dot(h, wg_ref[0])
    up = _dot(h, wu_ref[0])
    act = (gate / (1.0 + jnp.exp(-gate)) * up).astype(BF16)
    acc_ref[...] += _dot(act, wo_ref[0])

    if final:
        @pl.when(k == pl.num_programs(1) - 1)
        def _():
            acc_ref[...] = _rms(acc_ref[...], g_ref[...])


def _ffn(h2d, x2d, layer, w_in, w_out, g_final, final):
    n, d = x2d.shape
    f = w_out.shape[1]
    tm, tf = TM_FFN, TF_FFN
    nf = f // tf
    assert nf >= 2 and n % tm == 0 and f % tf == 0
    row = lambda i, k: (i, 0)
    return pl.pallas_call(
        functools.partial(_ffn_body, final),
        grid=(n // tm, nf),
        in_specs=[
            pl.BlockSpec((tm, d), row),
            pl.BlockSpec(memory_space=pl.ANY),
            pl.BlockSpec((1, d, tf), lambda i, k: (layer, 0, k)),
            pl.BlockSpec((1, d, tf), lambda i, k: (layer, 0, nf + k)),
            pl.BlockSpec((1, tf, d), lambda i, k: (layer, k, 0)),
            _const_spec((1, d)),
        ],
        out_specs=pl.BlockSpec((tm, d), row),
        out_shape=jax.ShapeDtypeStruct((n, d), F32),
        scratch_shapes=[pltpu.VMEM((tm, d), F32), pltpu.SemaphoreType.DMA(())],
        compiler_params=_params(("arbitrary", "arbitrary"), VMEM_LIMIT_FFN),
        name="ffn_final" if final else "ffn",
    )(h2d, x2d, w_in, w_in, w_out, g_final)


def _mla_proj_body(x_ref, gm_ref, pos_ref, win_ref, gq_ref, gkv_ref, wq_ref, wk_ref, wv_ref,
                   freq_ref, sign_ref, q_ref, k_ref, v_ref):
    ts = x_ref.shape[0]
    rq = gq_ref.shape[1]
    rkv = gkv_ref.shape[1]
    heads = v_ref.shape[1] // V_HEAD_DIM
    half = QK_ROPE_DIM // 2
    h = _rms(x_ref[...], gm_ref[0]).astype(BF16)
    c = _dot(h, win_ref[...])
    cq = _rms(c[:, :rq], gq_ref[...]).astype(BF16)
    ckv = _rms(c[:, rq:rq + rkv], gkv_ref[...]).astype(BF16)

    ang = pos_ref[...] * freq_ref[...]
    cos = jnp.cos(ang)
    sin_signed = jnp.sin(ang) * sign_ref[...]
    lane = lax.broadcasted_iota(jnp.int32, (ts, LANES), 1)

    def rope(x):
        swapped = jnp.where(lane < half,
                            pltpu.roll(x, LANES - half, axis=1),
                            pltpu.roll(x, half, axis=1))
        return x * cos + swapped * sin_signed

    k_rope = rope(c[:, rq + rkv:]).astype(BF16)
    group = 4
    for hc in range(heads // group):
        qc = _dot(cq, wq_ref[:, hc * group * QK_PAD_DIM:(hc + 1) * group * QK_PAD_DIM])
        for j in range(group):
            dst = (hc * group + j) * QK_PAD_DIM
            src = j * QK_PAD_DIM
            q_ref[:, dst:dst + QK_NOPE_DIM] = qc[:, src:src + QK_NOPE_DIM].astype(BF16)
            q_ref[:, dst + QK_NOPE_DIM:dst + QK_PAD_DIM] = rope(
                qc[:, src + QK_NOPE_DIM:src + QK_PAD_DIM]).astype(BF16)
    kn = _dot(ckv, wk_ref[...])
    for n in range(heads):
        dst = n * QK_PAD_DIM
        k_ref[:, dst:dst + QK_NOPE_DIM] = kn[:, n * QK_NOPE_DIM:(n + 1) * QK_NOPE_DIM].astype(BF16)
        k_ref[:, dst + QK_NOPE_DIM:dst + QK_PAD_DIM] = k_rope
    v_ref[...] = _dot(ckv, wv_ref[...]).astype(BF16)


def _mla_proj(x2d, gm, layer, pos, w_in, gq, gkv, wq, wk, wv, freq, sign):
    n, d = x2d.shape
    ts = TS_PROJ
    row = lambda i: (i, 0)
    return pl.pallas_call(
        _mla_proj_body,
        grid=(n // ts,),
        in_specs=[
            pl.BlockSpec((ts, d), row),
            _layer_spec(gm, layer),
            pl.BlockSpec((ts, 1), row),
            _const_spec(w_in.shape),
            _const_spec(gq.shape),
            _const_spec(gkv.shape),
            _const_spec(wq.shape),
            _const_spec(wk.shape),
            _const_spec(wv.shape),
            _const_spec(freq.shape),
            _const_spec(sign.shape),
        ],
        out_specs=[
            pl.BlockSpec((ts, wq.shape[1]), row),
            pl.BlockSpec((ts, wq.shape[1]), row),
            pl.BlockSpec((ts, wv.shape[1]), row),
        ],
        out_shape=[
            jax.ShapeDtypeStruct((n, wq.shape[1]), BF16),
            jax.ShapeDtypeStruct((n, wq.shape[1]), BF16),
            jax.ShapeDtypeStruct((n, wv.shape[1]), BF16),
        ],
        compiler_params=_params(("arbitrary",)),
        name="mla_proj",
    )(x2d, gm, pos, w_in, gq, gkv, wq, wk, wv, freq, sign)


def _attn_body(scale, q_ref, k_ref, v_ref, o_ref):
    seq = q_ref.shape[0]
    tq = TQ_ATTN
    row = lax.broadcasted_iota(jnp.int32, (tq, tq), 0)
    col = lax.broadcasted_iota(jnp.int32, (tq, tq), 1)
    causal = col <= row
    for qi in range(seq // tq):
        lo, hi = qi * tq, (qi + 1) * tq
        q = q_ref[lo:hi, :]
        s_diag = jnp.where(causal, _dot_nt(q, k_ref[lo:hi, :]) * scale, MASK_VALUE)
        m = jnp.max(s_diag, axis=-1, keepdims=True)
        if qi > 0:
            s_past = _dot_nt(q, k_ref[0:lo, :]) * scale
            m = jnp.maximum(m, jnp.max(s_past, axis=-1, keepdims=True))
        p_diag = jnp.exp(s_diag - m)
        l = jnp.sum(p_diag, axis=-1, keepdims=True)
        acc = _dot(p_diag.astype(BF16), v_ref[lo:hi, :])
        if qi > 0:
            p_past = jnp.exp(s_past - m)
            l = l + jnp.sum(p_past, axis=-1, keepdims=True)
            acc = acc + _dot(p_past.astype(BF16), v_ref[0:lo, :])
        o_ref[lo:hi, :] = (acc * (1.0 / l)).astype(BF16)


def _attention(q, k, v, batch, heads):
    n = q.shape[0]
    seq = n // batch
    scale = 1.0 / math.sqrt(QK_NOPE_DIM + QK_ROPE_DIM)
    bh = lambda b, h: (b, h)
    return pl.pallas_call(
        functools.partial(_attn_body, scale),
        grid=(batch, heads),
        in_specs=[
            pl.BlockSpec((seq, QK_PAD_DIM), bh),
            pl.BlockSpec((seq, QK_PAD_DIM), bh),
            pl.BlockSpec((seq, V_HEAD_DIM), bh),
        ],
        out_specs=pl.BlockSpec((seq, V_HEAD_DIM), bh),
        out_shape=jax.ShapeDtypeStruct((n, heads * V_HEAD_DIM), BF16),
        compiler_params=_params(("arbitrary", "arbitrary")),
        name="mla_attention",
    )(q, k, v)


def _pad_q_up(w):
    r = w.shape[0]
    heads = w.shape[1] // (QK_NOPE_DIM + QK_ROPE_DIM)
    w = w.reshape(r, heads, QK_NOPE_DIM + QK_ROPE_DIM)
    pad = jnp.zeros((r, heads, QK_PAD_DIM - QK_NOPE_DIM - QK_ROPE_DIM), w.dtype)
    return jnp.concatenate([w, pad], axis=-1).reshape(r, heads * QK_PAD_DIM).astype(BF16)


def _split_kv_up(w):
    r = w.shape[0]
    heads = w.shape[1] // (QK_NOPE_DIM + V_HEAD_DIM)
    w = w.reshape(r, heads, QK_NOPE_DIM + V_HEAD_DIM)
    wk = w[:, :, :QK_NOPE_DIM].reshape(r, heads * QK_NOPE_DIM)
    wv = w[:, :, QK_NOPE_DIM:].reshape(r, heads * V_HEAD_DIM)
    return wk.astype(BF16), wv.astype(BF16)


def _rope_rows():
    half = QK_ROPE_DIM // 2
    inv_freq = 1.0 / (ROPE_THETA ** (jnp.arange(0, QK_ROPE_DIM, 2, dtype=F32) / QK_ROPE_DIM))
    zeros = jnp.zeros((LANES - QK_ROPE_DIM,), F32)
    freq = jnp.concatenate([inv_freq, inv_freq, zeros]).reshape(1, LANES)
    sign = jnp.concatenate([-jnp.ones((half,), F32), jnp.ones((half,), F32), zeros]).reshape(1, LANES)
    return freq, sign


def kernel(x, mem, positions, norm_mix_g, norm_xattn_g, norm_mem_g, norm_ffn_g, pool_w, pool_scale, mla_w_in, mla_q_norm_g, mla_w_q_up, mla_kv_norm_g, mla_w_kv_up, mla_w_out, xattn_w_q, xattn_w_kv, xattn_w_o, ffn_w_in, ffn_w_out, final_norm_g):
    batch, seq, d = x.shape
    depth = norm_mix_g.shape[0]
    n_mixers = 2
    row = lambda g: g.reshape(1, -1)
    rows = lambda g: g.reshape(g.shape[0], 1, g.shape[1])

    gm, gx, gf = rows(norm_mix_g), rows(norm_xattn_g), rows(norm_ffn_g)
    wq_x, wo_x = xattn_w_q.astype(BF16), xattn_w_o.astype(BF16)
    w_ffn_in, w_ffn_out = ffn_w_in.astype(BF16), ffn_w_out.astype(BF16)
    w_pool, s_pool = pool_w.astype(BF16), rows(pool_scale)
    w_mla_out = mla_w_out.astype(BF16)

    k_mem, v_mem = _memkv(mem, norm_mem_g, xattn_w_kv.astype(BF16))
    freq, sign = _rope_rows()
    pos = positions.reshape(batch * seq, 1).astype(F32)

    x2d = x.reshape(batch * seq, d)
    for i in range(depth):
        j = i // n_mixers
        xattn_args = (gx, wq_x, k_mem, v_mem, wo_x, gf)
        if i % n_mixers == 0:
            x2d, h_ffn = _pool_xattn(x2d, batch, i, j, gm, w_pool, s_pool, *xattn_args)
        else:
            rq = mla_q_norm_g.shape[1]
            rkv = mla_kv_norm_g.shape[1]
            w_in = mla_w_in[j]
            w_in = jnp.pad(w_in, ((0, 0), (0, rq + rkv + LANES - w_in.shape[1]))).astype(BF16)
            wq = _pad_q_up(mla_w_q_up[j])
            wk, wv = _split_kv_up(mla_w_kv_up[j])
            heads = wv.shape[1] // V_HEAD_DIM
            q, k, v = _mla_proj(x2d, gm, i, pos, w_in, row(mla_q_norm_g[j]), row(mla_kv_norm_g[j]),
                                wq, wk, wv, freq, sign)
            o = _attention(q, k, v, batch, heads)
            x2d, h_ffn = _oproj_xattn(x2d, o, batch, i, j, w_mla_out, *xattn_args)
        x2d = _ffn(h_ffn, x2d, i, w_ffn_in, w_ffn_out, row(final_norm_g), i == depth - 1)
    return x2d.reshape(batch, seq, d)
```

```python
import functools
import math

import jax
import jax.numpy as jnp
from jax import lax
from jax.experimental import pallas as pl
from jax.experimental.pallas import tpu as pltpu

F32 = jnp.float32
BF16 = jnp.bfloat16

NORM_EPS = 1e-6
MASK_VALUE = -1e30
ROPE_THETA = 10000.0
POOL_WINDOWS = (2, 4, 8, 16)
QK_NOPE_DIM = 128
QK_ROPE_DIM = 64
V_HEAD_DIM = 128
XATTN_HEADS = 4

LANES = 128
QK_PAD_DIM = 256
POOL_HALO = 32

TS_MIX = 256
TS_PROJ = 512
TQ_ATTN = 256
HEADS_PER_ATTN_STEP = 2
TM_FFN = 1024
TF_FFN = 512

VMEM_LIMIT = 48 * 1024 * 1024
VMEM_LIMIT_FFN = 56 * 1024 * 1024


def _rms(x, g):
    var = jnp.mean(x * x, axis=-1, keepdims=True)
    return x * lax.rsqrt(var + NORM_EPS) * g


def _dot(a, b):
    return jnp.dot(a, b, preferred_element_type=F32)


def _dot_nt(a, b):
    return lax.dot_general(a, b, (((1,), (1,)), ((), ())), preferred_element_type=F32)


def _const_spec(shape):
    nd = len(shape)
    return pl.BlockSpec(shape, lambda *_: (0,) * nd)


def _layer_spec(stacked, layer):
    tail = stacked.shape[1:]
    return pl.BlockSpec((1,) + tail, lambda *_: (layer,) + (0,) * len(tail))


def _params(sem, vmem_limit=VMEM_LIMIT):
    return pltpu.CompilerParams(dimension_semantics=sem, vmem_limit_bytes=vmem_limit)


def _memkv_body(mem_ref, g_ref, w_ref, k_ref, v_ref):
    xw = k_ref.shape[-1]
    mn = _rms(mem_ref[0], g_ref[0]).astype(BF16)
    kv = _dot(mn, w_ref[0])
    k_ref[0, 0] = kv[:, :xw].astype(BF16)
    v_ref[0, 0] = kv[:, xw:].astype(BF16)


def _memkv(mem, g, w_kv):
    depth, d, xw2 = w_kv.shape
    b, m, _ = mem.shape
    xw = xw2 // 2
    out = jax.ShapeDtypeStruct((depth, b, m, xw), BF16)
    return pl.pallas_call(
        _memkv_body,
        grid=(depth, b),
        in_specs=[
            pl.BlockSpec((1, m, d), lambda l, i: (i, 0, 0)),
            pl.BlockSpec((1, 1, d), lambda l, i: (l, 0, 0)),
            pl.BlockSpec((1, d, xw2), lambda l, i: (l, 0, 0)),
        ],
        out_specs=[
            pl.BlockSpec((1, 1, m, xw), lambda l, i: (l, i, 0, 0)),
            pl.BlockSpec((1, 1, m, xw), lambda l, i: (l, i, 0, 0)),
        ],
        out_shape=[out, out],
        compiler_params=_params(("arbitrary", "arbitrary")),
        name="memkv",
    )(mem, g.reshape(depth, 1, d), w_kv)


def _xattn_tail(x1, gx_ref, wq_ref, k_ref, v_ref, wo_ref, gf_ref, x2_ref, h3_ref):
    xw = wq_ref.shape[-1]
    hd = xw // XATTN_HEADS
    scale = 1.0 / math.sqrt(hd)
    h2 = _rms(x1, gx_ref[0]).astype(BF16)
    q = _dot(h2, wq_ref[0])
    heads = []
    for n in range(XATTN_HEADS):
        sl = slice(n * hd, (n + 1) * hd)
        s = _dot_nt(q[:, sl].astype(BF16), k_ref[0, 0, :, sl]) * scale
        m = jnp.max(s, axis=-1, keepdims=True)
        p = jnp.exp(s - m)
        l = jnp.sum(p, axis=-1, keepdims=True)
        o = _dot(p.astype(BF16), v_ref[0, 0, :, sl]) * (1.0 / l)
        heads.append(o.astype(BF16))
    o = jnp.concatenate(heads, axis=-1)
    x2 = x1 + _dot(o, wo_ref[0])
    x2_ref[...] = x2
    h3_ref[...] = _rms(x2, gf_ref[0]).astype(BF16)


def _pool_xattn_body(x_ref, halo_ref, gm_ref, pw_ref, ps_ref,
                     gx_ref, wq_ref, k_ref, v_ref, wo_ref, gf_ref,
                     x2_ref, h3_ref, buf_a, buf_b, x1_ref):
    ts, d = x_ref.shape
    dg = d // len(POOL_WINDOWS)
    rows = ts + POOL_HALO
    i = pl.program_id(1)
    x = x_ref[...]
    hx = _rms(x, gm_ref[0])
    hh = _rms(halo_ref[...], gm_ref[0]) * (i > 0).astype(F32)
    t = i * ts + lax.broadcasted_iota(jnp.int32, (ts, 1), 0)
    for g, w in enumerate(POOL_WINDOWS):
        sl = slice(g * dg, (g + 1) * dg)
        buf_a[0:POOL_HALO, :] = hh[:, sl]
        buf_a[POOL_HALO:rows, :] = hx[:, sl]
        src, dst, lo, shift = buf_a, buf_b, 0, w // 2
        while shift > 1:
            lo += 8
            dst[lo:rows, :] = src[lo:rows, :] + src[lo - shift:rows - shift, :]
            src, dst, shift = dst, src, shift // 2
        wsum = src[POOL_HALO:rows, :] + src[POOL_HALO - 1:rows - 1, :]
        inv_cnt = 1.0 / jnp.minimum(t + 1, w).astype(F32)
        pooled = (wsum * inv_cnt - hx[:, sl]).astype(BF16)
        x1_ref[:, sl] = x[:, sl] + _dot(pooled, pw_ref[0, g]) * ps_ref[0, :, sl]
    _xattn_tail(x1_ref[...], gx_ref, wq_ref, k_ref, v_ref, wo_ref, gf_ref, x2_ref, h3_ref)


def _pool_xattn(x2d, batch, layer, pool_layer, gm, pool_w, pool_scale, gx, wq, k_mem, v_mem, wo, gf):
    n, d = x2d.shape
    seq = n // batch
    ts = TS_MIX
    nt = seq // ts
    hb = ts // POOL_HALO
    dg = pool_w.shape[-1]
    _, _, m, xw = k_mem.shape
    row = lambda b, i: (b * nt + i, 0)
    halo = lambda b, i: (jnp.maximum((b * nt + i) * hb - 1, 0), 0)
    memkv = lambda b, i: (layer, b, 0, 0)
    return pl.pallas_call(
        _pool_xattn_body,
        grid=(batch, nt),
        in_specs=[
            pl.BlockSpec((ts, d), row),
            pl.BlockSpec((POOL_HALO, d), halo),
            _layer_spec(gm, layer),
            _layer_spec(pool_w, pool_layer),
            _layer_spec(pool_scale, pool_layer),
            _layer_spec(gx, layer),
            _layer_spec(wq, layer),
            pl.BlockSpec((1, 1, m, xw), memkv),
            pl.BlockSpec((1, 1, m, xw), memkv),
            _layer_spec(wo, layer),
            _layer_spec(gf, layer),
        ],
        out_specs=[pl.BlockSpec((ts, d), row), pl.BlockSpec((ts, d), row)],
        out_shape=[jax.ShapeDtypeStruct((n, d), F32), jax.ShapeDtypeStruct((n, d), BF16)],
        scratch_shapes=[
            pltpu.VMEM((ts + POOL_HALO, dg), F32),
            pltpu.VMEM((ts + POOL_HALO, dg), F32),
            pltpu.VMEM((ts, d), F32),
        ],
        compiler_params=_params(("arbitrary", "arbitrary")),
        name="pool_xattn",
    )(x2d, x2d, gm, pool_w, pool_scale, gx, wq, k_mem, v_mem, wo, gf)


def _oproj_xattn_body(x_ref, o_ref, wout_ref,
                      gx_ref, wq_ref, k_ref, v_ref, wo_ref, gf_ref, x2_ref, h3_ref):
    x1 = x_ref[...] + _dot(o_ref[...], wout_ref[0])
    _xattn_tail(x1, gx_ref, wq_ref, k_ref, v_ref, wo_ref, gf_ref, x2_ref, h3_ref)


def _oproj_xattn(x2d, o2d, batch, layer, mla_layer, w_out, gx, wq, k_mem, v_mem, wo, gf):
    n, d = x2d.shape
    seq = n // batch
    ts = TS_MIX
    nt = seq // ts
    _, _, m, xw = k_mem.shape
    row = lambda b, i: (b * nt + i, 0)
    memkv = lambda b, i: (layer, b, 0, 0)
    return pl.pallas_call(
        _oproj_xattn_body,
        grid=(batch, nt),
        in_specs=[
            pl.BlockSpec((ts, d), row),
            pl.BlockSpec((ts, o2d.shape[1]), row),
            _layer_spec(w_out, mla_layer),
            _layer_spec(gx, layer),
            _layer_spec(wq, layer),
            pl.BlockSpec((1, 1, m, xw), memkv),
            pl.BlockSpec((1, 1, m, xw), memkv),
            _layer_spec(wo, layer),
            _layer_spec(gf, layer),
        ],
        out_specs=[pl.BlockSpec((ts, d), row), pl.BlockSpec((ts, d), row)],
        out_shape=[jax.ShapeDtypeStruct((n, d), F32), jax.ShapeDtypeStruct((n, d), BF16)],
        compiler_params=_params(("arbitrary", "arbitrary")),
        name="oproj_xattn",
    )(x2d, o2d, w_out, gx, wq, k_mem, v_mem, wo, gf)


def _ffn_body(final, h_ref, x_hbm, wg_ref, wu_ref, wo_ref, g_ref, acc_ref, xbuf, sem):
    i = pl.program_id(0)
    k = pl.program_id(1)
    tm = acc_ref.shape[0]

    def x_copy(tile):
        start = tile * tm
        if not isinstance(start, int):
            start = pl.multiple_of(start, tm)
        return pltpu.make_async_copy(x_hbm.at[pl.ds(start, tm), :], xbuf, sem)

    @pl.when(jnp.logical_and(i == 0, k == 0))
    def _():
        x_copy(0).start()

    @pl.when(k == 0)
    def _():
        x_copy(i).wait()
        acc_ref[...] = xbuf[...]

    @pl.when(jnp.logical_and(k == 1, i + 1 < pl.num_programs(0)))
    def _():
        x_copy(i + 1).start()

    h = h_ref[...]
    gate = _dot(h, wg_ref[0])
    up = _dot(h, wu_ref[0])
    act = (gate / (1.0 + jnp.exp(-gate)) * up).astype(BF16)
    acc_ref[...] += _dot(act, wo_ref[0])

    if final:
        @pl.when(k == pl.num_programs(1) - 1)
        def _():
            acc_ref[...] = _rms(acc_ref[...], g_ref[...])


def _ffn(h2d, x2d, layer, w_in, w_out, g_final, final):
    n, d = x2d.shape
    f = w_out.shape[1]
    tm, tf = TM_FFN, TF_FFN
    nf = f // tf
    assert nf >= 2 and n % tm == 0 and f % tf == 0
    row = lambda i, k: (i, 0)
    return pl.pallas_call(
        functools.partial(_ffn_body, final),
        grid=(n // tm, nf),
        in_specs=[
            pl.BlockSpec((tm, d), row),
            pl.BlockSpec(memory_space=pl.ANY),
            pl.BlockSpec((1, d, tf), lambda i, k: (layer, 0, k)),
            pl.BlockSpec((1, d, tf), lambda i, k: (layer, 0, nf + k)),
            pl.BlockSpec((1, tf, d), lambda i, k: (layer, k, 0)),
            _const_spec((1, d)),
        ],
        out_specs=pl.BlockSpec((tm, d), row),
        out_shape=jax.ShapeDtypeStruct((n, d), F32),
        scratch_shapes=[pltpu.VMEM((tm, d), F32), pltpu.SemaphoreType.DMA(())],
        compiler_params=_params(("arbitrary", "arbitrary"), VMEM_LIMIT_FFN),
        name="ffn_final" if final else "ffn",
    )(h2d, x2d, w_in, w_in, w_out, g_final)


def _mla_proj_body(x_ref, gm_ref, pos_ref, win_ref, gq_ref, gkv_ref, wq_ref, wk_ref, wv_ref,
                   freq_ref, sign_ref, q_ref, k_ref, v_ref):
    ts = x_ref.shape[0]
    rq = gq_ref.shape[1]
    rkv = gkv_ref.shape[1]
    heads = v_ref.shape[1] // V_HEAD_DIM
    half = QK_ROPE_DIM // 2
    h = _rms(x_ref[...], gm_ref[0]).astype(BF16)
    c = _dot(h, win_ref[...])
    cq = _rms(c[:, :rq], gq_ref[...]).astype(BF16)
    ckv = _rms(c[:, rq:rq + rkv], gkv_ref[...]).astype(BF16)

    ang = pos_ref[...] * freq_ref[...]
    cos = jnp.cos(ang)
    sin_signed = jnp.sin(ang) * sign_ref[...]
    lane = lax.broadcasted_iota(jnp.int32, (ts, LANES), 1)

    def rope(x):
        swapped = jnp.where(lane < half,
                            pltpu.roll(x, LANES - half, axis=1),
                            pltpu.roll(x, half, axis=1))
        return x * cos + swapped * sin_signed

    k_rope = rope(c[:, rq + rkv:]).astype(BF16)
    group = 4
    for hc in range(heads // group):
        qc = _dot(cq, wq_ref[:, hc * group * QK_PAD_DIM:(hc + 1) * group * QK_PAD_DIM])
        for j in range(group):
            dst = (hc * group + j) * QK_PAD_DIM
            src = j * QK_PAD_DIM
            q_ref[:, dst:dst + QK_NOPE_DIM] = qc[:, src:src + QK_NOPE_DIM].astype(BF16)
            q_ref[:, dst + QK_NOPE_DIM:dst + QK_PAD_DIM] = rope(
                qc[:, src + QK_NOPE_DIM:src + QK_PAD_DIM]).astype(BF16)
    kn = _dot(ckv, wk_ref[...])
    for n in range(heads):
        dst = n * QK_PAD_DIM
        k_ref[:, dst:dst + QK_NOPE_DIM] = kn[:, n * QK_NOPE_DIM:(n + 1) * QK_NOPE_DIM].astype(BF16)
        k_ref[:, dst + QK_NOPE_DIM:dst + QK_PAD_DIM] = k_rope
    v_ref[...] = _dot(ckv, wv_ref[...]).astype(BF16)


def _mla_proj(x2d, gm, layer, pos, w_in, gq, gkv, wq, wk, wv, freq, sign):
    n, d = x2d.shape
    ts = TS_PROJ
    row = lambda i: (i, 0)
    return pl.pallas_call(
        _mla_proj_body,
        grid=(n // ts,),
        in_specs=[
            pl.BlockSpec((ts, d), row),
            _layer_spec(gm, layer),
            pl.BlockSpec((ts, 1), row),
            _const_spec(w_in.shape),
            _const_spec(gq.shape),
            _const_spec(gkv.shape),
            _const_spec(wq.shape),
            _const_spec(wk.shape),
            _const_spec(wv.shape),
            _const_spec(freq.shape),
            _const_spec(sign.shape),
        ],
        out_specs=[
            pl.BlockSpec((ts, wq.shape[1]), row),
            pl.BlockSpec((ts, wq.shape[1]), row),
            pl.BlockSpec((ts, wv.shape[1]), row),
        ],
        out_shape=[
            jax.ShapeDtypeStruct((n, wq.shape[1]), BF16),
            jax.ShapeDtypeStruct((n, wq.shape[1]), BF16),
            jax.ShapeDtypeStruct((n, wv.shape[1]), BF16),
        ],
        compiler_params=_params(("arbitrary",)),
        name="mla_proj",
    )(x2d, gm, pos, w_in, gq, gkv, wq, wk, wv, freq, sign)


def _attn_body(scale, q_ref, k_ref, v_ref, o_ref, vaug, s_buf, p_buf):
    seq = q_ref.shape[0]
    tq = TQ_ATTN
    nq = seq // tq
    dk = QK_PAD_DIM
    dv = V_HEAD_DIM
    heads = v_ref.shape[1] // dv
    c = scale * math.log2(math.e)
    row = lax.broadcasted_iota(jnp.int32, (tq, tq), 0)
    col = lax.broadcasted_iota(jnp.int32, (tq, tq), 1)
    causal = col <= row
    for h in range(heads):
        vaug[:, 2 * h * dv:(2 * h + 1) * dv] = v_ref[:, h * dv:(h + 1) * dv]
        vaug[:, (2 * h + 1) * dv:(2 * h + 2) * dv] = jnp.ones((seq, dv), BF16)

    tiles = []
    for h in range(heads):
        order = range(nq) if h % 2 == 0 else range(nq - 1, -1, -1)
        tiles += [(h, qi) for qi in order]

    def scores(t):
        h, qi = tiles[t]
        nk = (qi + 1) * tq
        s_buf[t % 2, :, 0:nk] = _dot_nt(q_ref[qi * tq:nk, h * dk:(h + 1) * dk],
                                        k_ref[0:nk, h * dk:(h + 1) * dk])

    def softmax(t):
        _, qi = tiles[t]
        lo, nk, b = qi * tq, (qi + 1) * tq, t % 2
        s_diag = jnp.where(causal, s_buf[b, :, lo:nk], MASK_VALUE)
        m = jnp.max(s_diag, axis=-1, keepdims=True)
        if qi > 0:
            m = jnp.maximum(m, jnp.max(s_buf[b, :, 0:lo], axis=-1, keepdims=True))
        mc = m * c
        p_buf[b, :, lo:nk] = jnp.exp2(s_diag * c - mc).astype(BF16)
        if qi > 0:
            p_buf[b, :, 0:lo] = jnp.exp2(s_buf[b, :, 0:lo] * c - mc).astype(BF16)

    def weighted_values(t):
        h, qi = tiles[t]
        lo, nk, b = qi * tq, (qi + 1) * tq, t % 2
        acc = _dot(p_buf[b, :, 0:nk], vaug[0:nk, 2 * h * dv:(2 * h + 2) * dv])
        o_ref[lo:nk, h * dv:(h + 1) * dv] = (acc[:, 0:dv] / acc[:, dv:2 * dv]).astype(BF16)

    scores(0)
    for t in range(len(tiles)):
        if t + 1 < len(tiles):
            scores(t + 1)
        softmax(t)
        weighted_values(t)


def _attention(q, k, v, batch, heads):
    n = q.shape[0]
    seq = n // batch
    hp = HEADS_PER_ATTN_STEP
    scale = 1.0 / math.sqrt(QK_NOPE_DIM + QK_ROPE_DIM)
    bh = lambda b, h: (b, h)
    return pl.pallas_call(
        functools.partial(_attn_body, scale),
        grid=(batch, heads // hp),
        in_specs=[
            pl.BlockSpec((seq, hp * QK_PAD_DIM), bh),
            pl.BlockSpec((seq, hp * QK_PAD_DIM), bh),
            pl.BlockSpec((seq, hp * V_HEAD_DIM), bh),
        ],
        out_specs=pl.BlockSpec((seq, hp * V_HEAD_DIM), bh),
        out_shape=jax.ShapeDtypeStruct((n, heads * V_HEAD_DIM), BF16),
        scratch_shapes=[
            pltpu.VMEM((seq, 2 * hp * V_HEAD_DIM), BF16),
            pltpu.VMEM((2, TQ_ATTN, seq), F32),
            pltpu.VMEM((2, TQ_ATTN, seq), BF16),
        ],
        compiler_params=_params(("arbitrary", "arbitrary")),
        name="mla_attention",
    )(q, k, v)


def _pad_q_up(w):
    r = w.shape[0]
    heads = w.shape[1] // (QK_NOPE_DIM + QK_ROPE_DIM)
    w = w.reshape(r, heads, QK_NOPE_DIM + QK_ROPE_DIM)
    pad = jnp.zeros((r, heads, QK_PAD_DIM - QK_NOPE_DIM - QK_ROPE_DIM), w.dtype)
    return jnp.concatenate([w, pad], axis=-1).reshape(r, heads * QK_PAD_DIM).astype(BF16)


def _split_kv_up(w):
    r = w.shape[0]
    heads = w.shape[1] // (QK_NOPE_DIM + V_HEAD_DIM)
    w = w.reshape(r, heads, QK_NOPE_DIM + V_HEAD_DIM)
    wk = w[:, :, :QK_NOPE_DIM].reshape(r, heads * QK_NOPE_DIM)
    wv = w[:, :, QK_NOPE_DIM:].reshape(r, heads * V_HEAD_DIM)
    return wk.astype(BF16), wv.astype(BF16)


def _rope_rows():
    half = QK_ROPE_DIM // 2
    inv_freq = 1.0 / (ROPE_THETA ** (jnp.arange(0, QK_ROPE_DIM, 2, dtype=F32) / QK_ROPE_DIM))
    zeros = jnp.zeros((LANES - QK_ROPE_DIM,), F32)
    freq = jnp.concatenate([inv_freq, inv_freq, zeros]).reshape(1, LANES)
    sign = jnp.concatenate([-jnp.ones((half,), F32), jnp.ones((half,), F32), zeros]).reshape(1, LANES)
    return freq, sign


def kernel(x, mem, positions, norm_mix_g, norm_xattn_g, norm_mem_g, norm_ffn_g, pool_w, pool_scale, mla_w_in, mla_q_norm_g, mla_w_q_up, mla_kv_norm_g, mla_w_kv_up, mla_w_out, xattn_w_q, xattn_w_kv, xattn_w_o, ffn_w_in, ffn_w_out, final_norm_g):
    batch, seq, d = x.shape
    depth = norm_mix_g.shape[0]
    n_mixers = 2
    row = lambda g: g.reshape(1, -1)
    rows = lambda g: g.reshape(g.shape[0], 1, g.shape[1])

    gm, gx, gf = rows(norm_mix_g), rows(norm_xattn_g), rows(norm_ffn_g)
    wq_x, wo_x = xattn_w_q.astype(BF16), xattn_w_o.astype(BF16)
    w_ffn_in, w_ffn_out = ffn_w_in.astype(BF16), ffn_w_out.astype(BF16)
    w_pool, s_pool = pool_w.astype(BF16), rows(pool_scale)
    w_mla_out = mla_w_out.astype(BF16)

    k_mem, v_mem = _memkv(mem, norm_mem_g, xattn_w_kv.astype(BF16))
    freq, sign = _rope_rows()
    pos = positions.reshape(batch * seq, 1).astype(F32)

    x2d = x.reshape(batch * seq, d)
    for i in range(depth):
        j = i // n_mixers
        xattn_args = (gx, wq_x, k_mem, v_mem, wo_x, gf)
        if i % n_mixers == 0:
            x2d, h_ffn = _pool_xattn(x2d, batch, i, j, gm, w_pool, s_pool, *xattn_args)
        else:
            rq = mla_q_norm_g.shape[1]
            rkv = mla_kv_norm_g.shape[1]
            w_in = mla_w_in[j]
            w_in = jnp.pad(w_in, ((0, 0), (0, rq + rkv + LANES - w_in.shape[1]))).astype(BF16)
            wq = _pad_q_up(mla_w_q_up[j])
            wk, wv = _split_kv_up(mla_w_kv_up[j])
            heads = wv.shape[1] // V_HEAD_DIM
            q, k, v = _mla_proj(x2d, gm, i, pos, w_in, row(mla_q_norm_g[j]), row(mla_kv_norm_g[j]),
                                wq, wk, wv, freq, sign)
            o = _attention(q, k, v, batch, heads)
            x2d, h_ffn = _oproj_xattn(x2d, o, batch, i, j, w_mla_out, *xattn_args)
        x2d = _ffn(h_ffn, x2d, i, w_ffn_in, w_ffn_out, row(final_norm_g), i == depth - 1)
    return x2d.reshape(batch, seq, d)
```

```python
import functools
import math

import jax
import jax.numpy as jnp
from jax import lax
from jax.experimental import pallas as pl
from jax.experimental.pallas import tpu as pltpu

F32 = jnp.float32
BF16 = jnp.bfloat16

NORM_EPS = 1e-6
MASK_VALUE = -1e30
ROPE_THETA = 10000.0
POOL_WINDOWS = (2, 4, 8, 16)
QK_NOPE_DIM = 128
QK_ROPE_DIM = 64
V_HEAD_DIM = 128
XATTN_HEADS = 4

LANES = 128
QK_PAD_DIM = 256
POOL_HALO = 32

TS_MIX = 512
TS_PROJ = 512
TQ_ATTN = 256
HEADS_PER_ATTN_STEP = 4
TM_FFN = 1024
TF_FFN = 512

VMEM_LIMIT = 48 * 1024 * 1024
VMEM_LIMIT_FFN = 56 * 1024 * 1024


def _rms(x, g):
    var = jnp.mean(x * x, axis=-1, keepdims=True)
    return x * lax.rsqrt(var + NORM_EPS) * g


def _dot(a, b):
    return jnp.dot(a, b, preferred_element_type=F32)


def _dot_nt(a, b):
    return lax.dot_general(a, b, (((1,), (1,)), ((), ())), preferred_element_type=F32)


def _const_spec(shape):
    nd = len(shape)
    return pl.BlockSpec(shape, lambda *_: (0,) * nd)


def _layer_spec(stacked, layer):
    tail = stacked.shape[1:]
    return pl.BlockSpec((1,) + tail, lambda *_: (layer,) + (0,) * len(tail))


def _params(sem, vmem_limit=VMEM_LIMIT):
    return pltpu.CompilerParams(dimension_semantics=sem, vmem_limit_bytes=vmem_limit)


def _memkv_body(mem_ref, g_ref, w_ref, k_ref, v_ref):
    xw = k_ref.shape[-1]
    hd = xw // XATTN_HEADS
    mn = _rms(mem_ref[0], g_ref[0]).astype(BF16)
    kv = _dot(mn, w_ref[0])
    k_ref[0, 0] = kv[:, :xw].astype(BF16)
    ones = jnp.ones((kv.shape[0], hd), BF16)
    for n in range(XATTN_HEADS):
        v_ref[0, 0, :, 2 * n * hd:(2 * n + 1) * hd] = kv[:, xw + n * hd:xw + (n + 1) * hd].astype(BF16)
        v_ref[0, 0, :, (2 * n + 1) * hd:(2 * n + 2) * hd] = ones


def _memkv(mem, g, w_kv):
    depth, d, xw2 = w_kv.shape
    b, m, _ = mem.shape
    xw = xw2 // 2
    return pl.pallas_call(
        _memkv_body,
        grid=(depth, b),
        in_specs=[
            pl.BlockSpec((1, m, d), lambda l, i: (i, 0, 0)),
            pl.BlockSpec((1, 1, d), lambda l, i: (l, 0, 0)),
            pl.BlockSpec((1, d, xw2), lambda l, i: (l, 0, 0)),
        ],
        out_specs=[
            pl.BlockSpec((1, 1, m, xw), lambda l, i: (l, i, 0, 0)),
            pl.BlockSpec((1, 1, m, xw2), lambda l, i: (l, i, 0, 0)),
        ],
        out_shape=[jax.ShapeDtypeStruct((depth, b, m, xw), BF16),
                   jax.ShapeDtypeStruct((depth, b, m, xw2), BF16)],
        compiler_params=_params(("arbitrary", "arbitrary")),
        name="memkv",
    )(mem, g.reshape(depth, 1, d), w_kv)


def _xattn_tail(x1, gx_ref, wq_ref, k_ref, v_ref, wo_ref, gf_ref, x2_ref, h3_ref):
    xw = wq_ref.shape[-1]
    hd = xw // XATTN_HEADS
    c = math.log2(math.e) / math.sqrt(hd)
    h2 = _rms(x1, gx_ref[0]).astype(BF16)
    q = _dot(h2, wq_ref[0]).astype(BF16)
    heads = []
    for n in range(XATTN_HEADS):
        sl = slice(n * hd, (n + 1) * hd)
        s = _dot_nt(q[:, sl], k_ref[0, 0, :, sl])
        mc = jnp.max(s, axis=-1, keepdims=True) * c
        p = jnp.exp2(s * c - mc).astype(BF16)
        acc = _dot(p, v_ref[0, 0, :, 2 * n * hd:(2 * n + 2) * hd])
        heads.append((acc[:, :hd] / acc[:, hd:]).astype(BF16))
    o = jnp.concatenate(heads, axis=-1)
    x2 = x1 + _dot(o, wo_ref[0])
    x2_ref[...] = x2
    h3_ref[...] = _rms(x2, gf_ref[0]).astype(BF16)


def _pool_xattn_body(x_ref, halo_ref, gm_ref, pw_ref, ps_ref,
                     gx_ref, wq_ref, k_ref, v_ref, wo_ref, gf_ref,
                     x2_ref, h3_ref, buf_a, buf_b):
    ts, d = x_ref.shape
    dg = d // len(POOL_WINDOWS)
    rows = ts + POOL_HALO
    i = pl.program_id(1)
    x = x_ref[...]
    hx = _rms(x, gm_ref[0])
    hh = _rms(halo_ref[...], gm_ref[0]) * (i > 0).astype(F32)
    t = i * ts + lax.broadcasted_iota(jnp.int32, (ts, 1), 0)
    x1 = []
    for g, w in enumerate(POOL_WINDOWS):
        sl = slice(g * dg, (g + 1) * dg)
        buf_a[0:POOL_HALO, :] = hh[:, sl]
        buf_a[POOL_HALO:rows, :] = hx[:, sl]
        src, dst, lo, shift = buf_a, buf_b, 0, w // 2
        while shift > 1:
            lo += 8
            dst[lo:rows, :] = src[lo:rows, :] + src[lo - shift:rows - shift, :]
            src, dst, shift = dst, src, shift // 2
        wsum = src[POOL_HALO:rows, :] + src[POOL_HALO - 1:rows - 1, :]
        inv_cnt = 1.0 / jnp.minimum(t + 1, w).astype(F32)
        pooled = (wsum * inv_cnt - hx[:, sl]).astype(BF16)
        x1.append(x[:, sl] + _dot(pooled, pw_ref[0, g]) * ps_ref[0, :, sl])
    _xattn_tail(jnp.concatenate(x1, axis=-1),
                gx_ref, wq_ref, k_ref, v_ref, wo_ref, gf_ref, x2_ref, h3_ref)


def _pool_xattn(x2d, batch, layer, pool_layer, gm, pool_w, pool_scale, gx, wq, k_mem, v_mem, wo, gf):
    n, d = x2d.shape
    seq = n // batch
    ts = TS_MIX
    nt = seq // ts
    hb = ts // POOL_HALO
    dg = pool_w.shape[-1]
    _, _, m, xw = k_mem.shape
    row = lambda b, i: (b * nt + i, 0)
    halo = lambda b, i: (jnp.maximum((b * nt + i) * hb - 1, 0), 0)
    memkv = lambda b, i: (layer, b, 0, 0)
    return pl.pallas_call(
        _pool_xattn_body,
        grid=(batch, nt),
        in_specs=[
            pl.BlockSpec((ts, d), row),
            pl.BlockSpec((POOL_HALO, d), halo),
            _layer_spec(gm, layer),
            _layer_spec(pool_w, pool_layer),
            _layer_spec(pool_scale, pool_layer),
            _layer_spec(gx, layer),
            _layer_spec(wq, layer),
            pl.BlockSpec((1, 1, m, xw), memkv),
            pl.BlockSpec((1, 1, m, 2 * xw), memkv),
            _layer_spec(wo, layer),
            _layer_spec(gf, layer),
        ],
        out_specs=[pl.BlockSpec((ts, d), row), pl.BlockSpec((ts, d), row)],
        out_shape=[jax.ShapeDtypeStruct((n, d), F32), jax.ShapeDtypeStruct((n, d), BF16)],
        scratch_shapes=[
            pltpu.VMEM((ts + POOL_HALO, dg), F32),
            pltpu.VMEM((ts + POOL_HALO, dg), F32),
        ],
        compiler_params=_params(("arbitrary", "arbitrary")),
        name="pool_xattn",
    )(x2d, x2d, gm, pool_w, pool_scale, gx, wq, k_mem, v_mem, wo, gf)


def _oproj_xattn_body(x_ref, o_ref, wout_ref,
                      gx_ref, wq_ref, k_ref, v_ref, wo_ref, gf_ref, x2_ref, h3_ref):
    x1 = x_ref[...] + _dot(o_ref[...], wout_ref[0])
    _xattn_tail(x1, gx_ref, wq_ref, k_ref, v_ref, wo_ref, gf_ref, x2_ref, h3_ref)


def _oproj_xattn(x2d, o2d, batch, layer, mla_layer, w_out, gx, wq, k_mem, v_mem, wo, gf):
    n, d = x2d.shape
    seq = n // batch
    ts = TS_MIX
    nt = seq // ts
    _, _, m, xw = k_mem.shape
    row = lambda b, i: (b * nt + i, 0)
    memkv = lambda b, i: (layer, b, 0, 0)
    return pl.pallas_call(
        _oproj_xattn_body,
        grid=(batch, nt),
        in_specs=[
            pl.BlockSpec((ts, d), row),
            pl.BlockSpec((ts, o2d.shape[1]), row),
            _layer_spec(w_out, mla_layer),
            _layer_spec(gx, layer),
            _layer_spec(wq, layer),
            pl.BlockSpec((1, 1, m, xw), memkv),
            pl.BlockSpec((1, 1, m, 2 * xw), memkv),
            _layer_spec(wo, layer),
            _layer_spec(gf, layer),
        ],
        out_specs=[pl.BlockSpec((ts, d), row), pl.BlockSpec((ts, d), row)],
        out_shape=[jax.ShapeDtypeStruct((n, d), F32), jax.ShapeDtypeStruct((n, d), BF16)],
        compiler_params=_params(("arbitrary", "arbitrary")),
        name="oproj_xattn",
    )(x2d, o2d, w_out, gx, wq, k_mem, v_mem, wo, gf)


def _ffn_body(final, prefetch_step, h_ref, x_hbm, wg_ref, wu_ref, wo_ref, g_ref, acc_ref, xbuf, sem):
    i = pl.program_id(0)
    k = pl.program_id(1)
    tm = acc_ref.shape[0]

    def x_copy(tile):
        start = tile * tm
        if not isinstance(start, int):
            start = pl.multiple_of(start, tm)
        return pltpu.make_async_copy(x_hbm.at[pl.ds(start, tm), :], xbuf, sem)

    @pl.when(jnp.logical_and(i == 0, k == 0))
    def _():
        x_copy(0).start()

    @pl.when(k == 0)
    def _():
        x_copy(i).wait()
        acc_ref[...] = xbuf[...]

    @pl.when(jnp.logical_and(k == prefetch_step, i + 1 < pl.num_programs(0)))
    def _():
        x_copy(i + 1).start()

    h = h_ref[...]
    gate = _dot(h, wg_ref[0])
    up = _dot(h, wu_ref[0])
    act = (gate / (1.0 + jnp.exp(-gate)) * up).astype(BF16)
    acc_ref[...] += _dot(act, wo_ref[0])

    if final:
        @pl.when(k == pl.num_programs(1) - 1)
        def _():
            acc_ref[...] = _rms(acc_ref[...], g_ref[...])


def _ffn(h2d, x2d, layer, w_in, w_out, g_final, final):
    n, d = x2d.shape
    f = w_out.shape[1]
    tm, tf = TM_FFN, TF_FFN
    nf = f // tf
    assert nf >= 2 and n % tm == 0 and f % tf == 0
    row = lambda i, k: (i, 0)
    return pl.pallas_call(
        functools.partial(_ffn_body, final, nf // 2),
        grid=(n // tm, nf),
        in_specs=[
            pl.BlockSpec((tm, d), row),
            pl.BlockSpec(memory_space=pl.ANY),
            pl.BlockSpec((1, d, tf), lambda i, k: (layer, 0, k)),
            pl.BlockSpec((1, d, tf), lambda i, k: (layer, 0, nf + k)),
            pl.BlockSpec((1, tf, d), lambda i, k: (layer, k, 0)),
            _const_spec((1, d)),
        ],
        out_specs=pl.BlockSpec((tm, d), row),
        out_shape=jax.ShapeDtypeStruct((n, d), F32),
        scratch_shapes=[pltpu.VMEM((tm, d), F32), pltpu.SemaphoreType.DMA(())],
        compiler_params=_params(("arbitrary", "arbitrary"), VMEM_LIMIT_FFN),
        name="ffn_final" if final else "ffn",
    )(h2d, x2d, w_in, w_in, w_out, g_final)


def _mla_proj_body(x_ref, gm_ref, pos_ref, win_ref, gq_ref, gkv_ref, wqn_ref, wqr_ref, wk_ref, wv_ref,
                   freq_ref, sign_ref, q_ref, k_ref, v_ref):
    ts = x_ref.shape[0]
    rq = gq_ref.shape[1]
    rkv = gkv_ref.shape[1]
    heads = v_ref.shape[1] // V_HEAD_DIM
    half = QK_ROPE_DIM // 2
    h = _rms(x_ref[...], gm_ref[0]).astype(BF16)
    c = _dot(h, win_ref[...])
    cq = _rms(c[:, :rq], gq_ref[...]).astype(BF16)
    ckv = _rms(c[:, rq:rq + rkv], gkv_ref[...]).astype(BF16)

    ang = pos_ref[...] * freq_ref[...]
    cos = jnp.cos(ang)
    sin_signed = jnp.sin(ang) * sign_ref[...]
    lane = lax.broadcasted_iota(jnp.int32, (ts, LANES), 1)
    first_half = (lane & (QK_ROPE_DIM - 1)) < half
    low_lanes = lane < QK_ROPE_DIM

    def rope(x):
        swapped = jnp.where(first_half,
                            pltpu.roll(x, LANES - half, axis=1),
                            pltpu.roll(x, half, axis=1))
        return x * cos + swapped * sin_signed

    k_rope = rope(c[:, rq + rkv:]).astype(BF16)
    qr = _dot(cq, wqr_ref[...])
    for pair in range(heads // 2):
        r = rope(qr[:, pair * LANES:(pair + 1) * LANES])
        for j, part in enumerate((r, pltpu.roll(r, QK_ROPE_DIM, axis=1))):
            dst = (2 * pair + j) * QK_PAD_DIM + QK_NOPE_DIM
            q_ref[:, dst:dst + LANES] = jnp.where(low_lanes, part, 0.0).astype(BF16)
    group = 4
    for hc in range(heads // group):
        qn = _dot(cq, wqn_ref[:, hc * group * QK_NOPE_DIM:(hc + 1) * group * QK_NOPE_DIM])
        for j in range(group):
            dst = (hc * group + j) * QK_PAD_DIM
            q_ref[:, dst:dst + QK_NOPE_DIM] = qn[:, j * QK_NOPE_DIM:(j + 1) * QK_NOPE_DIM].astype(BF16)
    kn = _dot(ckv, wk_ref[...])
    for n in range(heads):
        dst = n * QK_PAD_DIM
        k_ref[:, dst:dst + QK_NOPE_DIM] = kn[:, n * QK_NOPE_DIM:(n + 1) * QK_NOPE_DIM].astype(BF16)
        k_ref[:, dst + QK_NOPE_DIM:dst + QK_PAD_DIM] = k_rope
    v_ref[...] = _dot(ckv, wv_ref[...]).astype(BF16)


def _mla_proj(x2d, gm, layer, pos, w_in, gq, gkv, wqn, wqr, wk, wv, freq, sign):
    n, d = x2d.shape
    ts = TS_PROJ
    heads = wv.shape[1] // V_HEAD_DIM
    qk_width = heads * QK_PAD_DIM
    row = lambda i: (i, 0)
    return pl.pallas_call(
        _mla_proj_body,
        grid=(n // ts,),
        in_specs=[
            pl.BlockSpec((ts, d), row),
            _layer_spec(gm, layer),
            pl.BlockSpec((ts, 1), row),
            _const_spec(w_in.shape),
            _const_spec(gq.shape),
            _const_spec(gkv.shape),
            _const_spec(wqn.shape),
            _const_spec(wqr.shape),
            _const_spec(wk.shape),
            _const_spec(wv.shape),
            _const_spec(freq.shape),
            _const_spec(sign.shape),
        ],
        out_specs=[
            pl.BlockSpec((ts, qk_width), row),
            pl.BlockSpec((ts, qk_width), row),
            pl.BlockSpec((ts, wv.shape[1]), row),
        ],
        out_shape=[
            jax.ShapeDtypeStruct((n, qk_width), BF16),
            jax.ShapeDtypeStruct((n, qk_width), BF16),
            jax.ShapeDtypeStruct((n, wv.shape[1]), BF16),
        ],
        compiler_params=_params(("arbitrary",)),
        name="mla_proj",
    )(x2d, gm, pos, w_in, gq, gkv, wqn, wqr, wk, wv, freq, sign)


def _attn_body(scale, q_ref, k_ref, v_ref, o_ref, vaug, s_buf, p_buf):
    seq = q_ref.shape[0]
    tq = TQ_ATTN
    nq = seq // tq
    dk = QK_PAD_DIM
    dv = V_HEAD_DIM
    heads = v_ref.shape[1] // dv
    c = scale * math.log2(math.e)
    row = lax.broadcasted_iota(jnp.int32, (tq, tq), 0)
    col = lax.broadcasted_iota(jnp.int32, (tq, tq), 1)
    causal = col <= row
    for h in range(heads):
        vaug[:, 2 * h * dv:(2 * h + 1) * dv] = v_ref[:, h * dv:(h + 1) * dv]
        vaug[:, (2 * h + 1) * dv:(2 * h + 2) * dv] = jnp.ones((seq, dv), BF16)

    tiles = []
    for h in range(heads):
        order = range(nq) if h % 2 == 0 else range(nq - 1, -1, -1)
        tiles += [(h, qi) for qi in order]

    def scores(t):
        h, qi = tiles[t]
        nk = (qi + 1) * tq
        s_buf[t % 2, :, 0:nk] = _dot_nt(q_ref[qi * tq:nk, h * dk:(h + 1) * dk],
                                        k_ref[0:nk, h * dk:(h + 1) * dk])

    def softmax(t):
        _, qi = tiles[t]
        lo, nk, b = qi * tq, (qi + 1) * tq, t % 2
        s_diag = jnp.where(causal, s_buf[b, :, lo:nk], MASK_VALUE)
        m = jnp.max(s_diag, axis=-1, keepdims=True)
        if qi > 0:
            m = jnp.maximum(m, jnp.max(s_buf[b, :, 0:lo], axis=-1, keepdims=True))
        mc = m * c
        p_buf[b, :, lo:nk] = jnp.exp2(s_diag * c - mc).astype(BF16)
        if qi > 0:
            p_buf[b, :, 0:lo] = jnp.exp2(s_buf[b, :, 0:lo] * c - mc).astype(BF16)

    def weighted_values(t):
        h, qi = tiles[t]
        lo, nk, b = qi * tq, (qi + 1) * tq, t % 2
        acc = _dot(p_buf[b, :, 0:nk], vaug[0:nk, 2 * h * dv:(2 * h + 2) * dv])
        o_ref[lo:nk, h * dv:(h + 1) * dv] = (acc[:, 0:dv] / acc[:, dv:2 * dv]).astype(BF16)

    scores(0)
    for t in range(len(tiles)):
        if t + 1 < len(tiles):
            scores(t + 1)
        softmax(t)
        weighted_values(t)


def _attention(q, k, v, batch, heads):
    n = q.shape[0]
    seq = n // batch
    hp = HEADS_PER_ATTN_STEP
    scale = 1.0 / math.sqrt(QK_NOPE_DIM + QK_ROPE_DIM)
    bh = lambda b, h: (b, h)
    return pl.pallas_call(
        functools.partial(_attn_body, scale),
        grid=(batch, heads // hp),
        in_specs=[
            pl.BlockSpec((seq, hp * QK_PAD_DIM), bh),
            pl.BlockSpec((seq, hp * QK_PAD_DIM), bh),
            pl.BlockSpec((seq, hp * V_HEAD_DIM), bh),
        ],
        out_specs=pl.BlockSpec((seq, hp * V_HEAD_DIM), bh),
        out_shape=jax.ShapeDtypeStruct((n, heads * V_HEAD_DIM), BF16),
        scratch_shapes=[
            pltpu.VMEM((seq, 2 * hp * V_HEAD_DIM), BF16),
            pltpu.VMEM((2, TQ_ATTN, seq), F32),
            pltpu.VMEM((2, TQ_ATTN, seq), BF16),
        ],
        compiler_params=_params(("arbitrary", "arbitrary")),
        name="mla_attention",
    )(q, k, v)


def _split_head_columns(w, head_dim, split):
    r = w.shape[0]
    heads = w.shape[1] // head_dim
    w = w.astype(BF16).reshape(r, heads, head_dim)
    a = w[:, :, :split].reshape(r, heads * split)
    b = w[:, :, split:].reshape(r, heads * (head_dim - split))
    return a, b


def _rope_rows():
    half = QK_ROPE_DIM // 2
    inv_freq = 1.0 / (ROPE_THETA ** (jnp.arange(0, QK_ROPE_DIM, 2, dtype=F32) / QK_ROPE_DIM))
    groups = LANES // QK_ROPE_DIM
    freq = jnp.tile(jnp.concatenate([inv_freq, inv_freq]), groups).reshape(1, LANES)
    sign = jnp.tile(jnp.concatenate([-jnp.ones((half,), F32), jnp.ones((half,), F32)]),
                    groups).reshape(1, LANES)
    return freq, sign


def kernel(x, mem, positions, norm_mix_g, norm_xattn_g, norm_mem_g, norm_ffn_g, pool_w, pool_scale, mla_w_in, mla_q_norm_g, mla_w_q_up, mla_kv_norm_g, mla_w_kv_up, mla_w_out, xattn_w_q, xattn_w_kv, xattn_w_o, ffn_w_in, ffn_w_out, final_norm_g):
    batch, seq, d = x.shape
    depth = norm_mix_g.shape[0]
    n_mixers = 2
    row = lambda g: g.reshape(1, -1)
    rows = lambda g: g.reshape(g.shape[0], 1, g.shape[1])

    gm, gx, gf = rows(norm_mix_g), rows(norm_xattn_g), rows(norm_ffn_g)
    wq_x, wo_x = xattn_w_q.astype(BF16), xattn_w_o.astype(BF16)
    w_ffn_in, w_ffn_out = ffn_w_in.astype(BF16), ffn_w_out.astype(BF16)
    w_pool, s_pool = pool_w.astype(BF16), rows(pool_scale)
    w_mla_out = mla_w_out.astype(BF16)

    k_mem, v_mem = _memkv(mem, norm_mem_g, xattn_w_kv.astype(BF16))
    freq, sign = _rope_rows()
    pos = positions.reshape(batch * seq, 1).astype(F32)

    x2d = x.reshape(batch * seq, d)
    for i in range(depth):
        j = i // n_mixers
        xattn_args = (gx, wq_x, k_mem, v_mem, wo_x, gf)
        if i % n_mixers == 0:
            x2d, h_ffn = _pool_xattn(x2d, batch, i, j, gm, w_pool, s_pool, *xattn_args)
        else:
            rq = mla_q_norm_g.shape[1]
            rkv = mla_kv_norm_g.shape[1]
            w_in = mla_w_in[j].astype(BF16)
            w_in = jnp.pad(w_in, ((0, 0), (0, rq + rkv + LANES - w_in.shape[1])))
            wqn, wqr = _split_head_columns(mla_w_q_up[j], QK_NOPE_DIM + QK_ROPE_DIM, QK_NOPE_DIM)
            wk, wv = _split_head_columns(mla_w_kv_up[j], QK_NOPE_DIM + V_HEAD_DIM, QK_NOPE_DIM)
            heads = wv.shape[1] // V_HEAD_DIM
            q, k, v = _mla_proj(x2d, gm, i, pos, w_in, row(mla_q_norm_g[j]), row(mla_kv_norm_g[j]),
                                wqn, wqr, wk, wv, freq, sign)
            o = _attention(q, k, v, batch, heads)
            x2d, h_ffn = _oproj_xattn(x2d, o, batch, i, j, w_mla_out, *xattn_args)
        x2d = _ffn(h_ffn, x2d, i, w_ffn_in, w_ffn_out, row(final_norm_g), i == depth - 1)
    return x2d.reshape(batch, seq, d)
```

```python
import functools
import math

import jax
import jax.numpy as jnp
from jax import lax
from jax.experimental import pallas as pl
from jax.experimental.pallas import tpu as pltpu

F32 = jnp.float32
BF16 = jnp.bfloat16

NORM_EPS = 1e-6
MASK_VALUE = -1e30
ROPE_THETA = 10000.0
POOL_WINDOWS = (2, 4, 8, 16)
QK_NOPE_DIM = 128
QK_ROPE_DIM = 64
V_HEAD_DIM = 128
XATTN_HEADS = 4

LANES = 128
QK_PAD_DIM = 256
POOL_HALO = 32

TS_MIX = 512
TS_PROJ = 512
TQ_ATTN = 256
HEADS_PER_ATTN_STEP = 2
SOFTMAX_CHUNK = 256
TM_FFN = 1024
TF_FFN = 512

VMEM_LIMIT = 48 * 1024 * 1024
VMEM_LIMIT_FFN = 56 * 1024 * 1024


def _rms(x, g):
    var = jnp.mean(x * x, axis=-1, keepdims=True)
    return x * lax.rsqrt(var + NORM_EPS) * g


def _dot(a, b):
    return jnp.dot(a, b, preferred_element_type=F32)


def _dot_nt(a, b):
    return lax.dot_general(a, b, (((1,), (1,)), ((), ())), preferred_element_type=F32)


def _const_spec(shape):
    nd = len(shape)
    return pl.BlockSpec(shape, lambda *_: (0,) * nd)


def _layer_spec(stacked, layer):
    tail = stacked.shape[1:]
    return pl.BlockSpec((1,) + tail, lambda *_: (layer,) + (0,) * len(tail))


def _params(sem, vmem_limit=VMEM_LIMIT):
    return pltpu.CompilerParams(dimension_semantics=sem, vmem_limit_bytes=vmem_limit)


def _memkv_body(mem_ref, g_ref, w_ref, k_ref, v_ref):
    xw = k_ref.shape[-1]
    hd = xw // XATTN_HEADS
    mn = _rms(mem_ref[0], g_ref[0]).astype(BF16)
    kv = _dot(mn, w_ref[0])
    k_ref[0, 0] = kv[:, :xw].astype(BF16)
    ones = jnp.ones((kv.shape[0], hd), BF16)
    for n in range(XATTN_HEADS):
        v_ref[0, 0, :, 2 * n * hd:(2 * n + 1) * hd] = kv[:, xw + n * hd:xw + (n + 1) * hd].astype(BF16)
        v_ref[0, 0, :, (2 * n + 1) * hd:(2 * n + 2) * hd] = ones


def _memkv(mem, g, w_kv):
    depth, d, xw2 = w_kv.shape
    b, m, _ = mem.shape
    xw = xw2 // 2
    return pl.pallas_call(
        _memkv_body,
        grid=(depth, b),
        in_specs=[
            pl.BlockSpec((1, m, d), lambda l, i: (i, 0, 0)),
            pl.BlockSpec((1, 1, d), lambda l, i: (l, 0, 0)),
            pl.BlockSpec((1, d, xw2), lambda l, i: (l, 0, 0)),
        ],
        out_specs=[
            pl.BlockSpec((1, 1, m, xw), lambda l, i: (l, i, 0, 0)),
            pl.BlockSpec((1, 1, m, xw2), lambda l, i: (l, i, 0, 0)),
        ],
        out_shape=[jax.ShapeDtypeStruct((depth, b, m, xw), BF16),
                   jax.ShapeDtypeStruct((depth, b, m, xw2), BF16)],
        compiler_params=_params(("arbitrary", "arbitrary")),
        name="memkv",
    )(mem, g.reshape(depth, 1, d), w_kv)


def _xattn_tail(x1, gx_ref, wq_ref, k_ref, v_ref, wo_ref, gf_ref, x2_ref, h3_ref):
    xw = wq_ref.shape[-1]
    hd = xw // XATTN_HEADS
    c = math.log2(math.e) / math.sqrt(hd)
    h2 = _rms(x1, gx_ref[0]).astype(BF16)
    q = _dot(h2, wq_ref[0]).astype(BF16)
    heads = []
    for n in range(XATTN_HEADS):
        sl = slice(n * hd, (n + 1) * hd)
        s = _dot_nt(q[:, sl], k_ref[0, 0, :, sl])
        mc = jnp.max(s, axis=-1, keepdims=True) * c
        p = jnp.exp2(s * c - mc).astype(BF16)
        acc = _dot(p, v_ref[0, 0, :, 2 * n * hd:(2 * n + 2) * hd])
        heads.append((acc[:, :hd] / acc[:, hd:]).astype(BF16))
    o = jnp.concatenate(heads, axis=-1)
    x2 = x1 + _dot(o, wo_ref[0])
    x2_ref[...] = x2
    h3_ref[...] = _rms(x2, gf_ref[0]).astype(BF16)


def _pool_xattn_body(x_ref, halo_ref, gm_ref, pw_ref, ps_ref,
                     gx_ref, wq_ref, k_ref, v_ref, wo_ref, gf_ref,
                     x2_ref, h3_ref, buf_a, buf_b):
    ts, d = x_ref.shape
    dg = d // len(POOL_WINDOWS)
    rows = ts + POOL_HALO
    i = pl.program_id(1)
    x = x_ref[...]
    hx = _rms(x, gm_ref[0])
    hh = _rms(halo_ref[...], gm_ref[0]) * (i > 0).astype(F32)
    t = i * ts + lax.broadcasted_iota(jnp.int32, (ts, 1), 0)
    x1 = []
    for g, w in enumerate(POOL_WINDOWS):
        sl = slice(g * dg, (g + 1) * dg)
        buf_a[0:POOL_HALO, :] = hh[:, sl]
        buf_a[POOL_HALO:rows, :] = hx[:, sl]
        src, dst, lo, shift = buf_a, buf_b, 0, w // 2
        while shift > 1:
            lo += 8
            dst[lo:rows, :] = src[lo:rows, :] + src[lo - shift:rows - shift, :]
            src, dst, shift = dst, src, shift // 2
        wsum = src[POOL_HALO:rows, :] + src[POOL_HALO - 1:rows - 1, :]
        inv_cnt = 1.0 / jnp.minimum(t + 1, w).astype(F32)
        pooled = (wsum * inv_cnt - hx[:, sl]).astype(BF16)
        x1.append(x[:, sl] + _dot(pooled, pw_ref[0, g]) * ps_ref[0, :, sl])
    _xattn_tail(jnp.concatenate(x1, axis=-1),
                gx_ref, wq_ref, k_ref, v_ref, wo_ref, gf_ref, x2_ref, h3_ref)


def _pool_xattn(x2d, batch, layer, pool_layer, gm, pool_w, pool_scale, gx, wq, k_mem, v_mem, wo, gf):
    n, d = x2d.shape
    seq = n // batch
    ts = TS_MIX
    nt = seq // ts
    hb = ts // POOL_HALO
    dg = pool_w.shape[-1]
    _, _, m, xw = k_mem.shape
    row = lambda b, i: (b * nt + i, 0)
    halo = lambda b, i: (jnp.maximum((b * nt + i) * hb - 1, 0), 0)
    memkv = lambda b, i: (layer, b, 0, 0)
    return pl.pallas_call(
        _pool_xattn_body,
        grid=(batch, nt),
        in_specs=[
            pl.BlockSpec((ts, d), row),
            pl.BlockSpec((POOL_HALO, d), halo),
            _layer_spec(gm, layer),
            _layer_spec(pool_w, pool_layer),
            _layer_spec(pool_scale, pool_layer),
            _layer_spec(gx, layer),
            _layer_spec(wq, layer),
            pl.BlockSpec((1, 1, m, xw), memkv),
            pl.BlockSpec((1, 1, m, 2 * xw), memkv),
            _layer_spec(wo, layer),
            _layer_spec(gf, layer),
        ],
        out_specs=[pl.BlockSpec((ts, d), row), pl.BlockSpec((ts, d), row)],
        out_shape=[jax.ShapeDtypeStruct((n, d), F32), jax.ShapeDtypeStruct((n, d), BF16)],
        scratch_shapes=[
            pltpu.VMEM((ts + POOL_HALO, dg), F32),
            pltpu.VMEM((ts + POOL_HALO, dg), F32),
        ],
        compiler_params=_params(("arbitrary", "arbitrary")),
        name="pool_xattn",
    )(x2d, x2d, gm, pool_w, pool_scale, gx, wq, k_mem, v_mem, wo, gf)


def _oproj_xattn_body(x_ref, o_ref, wout_ref,
                      gx_ref, wq_ref, k_ref, v_ref, wo_ref, gf_ref, x2_ref, h3_ref):
    x1 = x_ref[...] + _dot(o_ref[...], wout_ref[0])
    _xattn_tail(x1, gx_ref, wq_ref, k_ref, v_ref, wo_ref, gf_ref, x2_ref, h3_ref)


def _oproj_xattn(x2d, o2d, batch, layer, mla_layer, w_out, gx, wq, k_mem, v_mem, wo, gf):
    n, d = x2d.shape
    seq = n // batch
    ts = TS_MIX
    nt = seq // ts
    _, _, m, xw = k_mem.shape
    row = lambda b, i: (b * nt + i, 0)
    memkv = lambda b, i: (layer, b, 0, 0)
    return pl.pallas_call(
        _oproj_xattn_body,
        grid=(batch, nt),
        in_specs=[
            pl.BlockSpec((ts, d), row),
            pl.BlockSpec((ts, o2d.shape[1]), row),
            _layer_spec(w_out, mla_layer),
            _layer_spec(gx, layer),
            _layer_spec(wq, layer),
            pl.BlockSpec((1, 1, m, xw), memkv),
            pl.BlockSpec((1, 1, m, 2 * xw), memkv),
            _layer_spec(wo, layer),
            _layer_spec(gf, layer),
        ],
        out_specs=[pl.BlockSpec((ts, d), row), pl.BlockSpec((ts, d), row)],
        out_shape=[jax.ShapeDtypeStruct((n, d), F32), jax.ShapeDtypeStruct((n, d), BF16)],
        compiler_params=_params(("arbitrary", "arbitrary")),
        name="oproj_xattn",
    )(x2d, o2d, w_out, gx, wq, k_mem, v_mem, wo, gf)


def _ffn_body(final, prefetch_step, h_ref, x_hbm, wg_ref, wu_ref, wo_ref, g_ref, acc_ref, xbuf, sem):
    i = pl.program_id(0)
    k = pl.program_id(1)
    tm = acc_ref.shape[0]

    def x_copy(tile):
        start = tile * tm
        if not isinstance(start, int):
            start = pl.multiple_of(start, tm)
        return pltpu.make_async_copy(x_hbm.at[pl.ds(start, tm), :], xbuf, sem)

    @pl.when(jnp.logical_and(i == 0, k == 0))
    def _():
        x_copy(0).start()

    @pl.when(k == 0)
    def _():
        x_copy(i).wait()
        acc_ref[...] = xbuf[...]

    @pl.when(jnp.logical_and(k == prefetch_step, i + 1 < pl.num_programs(0)))
    def _():
        x_copy(i + 1).start()

    h = h_ref[...]
    gate = _dot(h, wg_ref[0, 0])
    up = _dot(h, wu_ref[0, 0])
    act = (gate / (1.0 + jnp.exp(-gate)) * up).astype(BF16)
    acc_ref[...] += _dot(act, wo_ref[0])

    if final:
        @pl.when(k == pl.num_programs(1) - 1)
        def _():
            acc_ref[...] = _rms(acc_ref[...], g_ref[...])


def _chunk_major(w_in, tf):
    layers, d, f2 = w_in.shape
    return w_in.reshape(layers, d, f2 // tf, tf).transpose(0, 2, 1, 3)


def _ffn(h2d, x2d, layer, w_in, w_out, g_final, final):
    n, d = x2d.shape
    f = w_out.shape[1]
    tm, tf = TM_FFN, TF_FFN
    nf = f // tf
    assert w_in.shape[1:] == (2 * nf, d, tf)
    assert nf >= 2 and n % tm == 0 and f % tf == 0
    row = lambda i, k: (i, 0)
    return pl.pallas_call(
        functools.partial(_ffn_body, final, nf // 2),
        grid=(n // tm, nf),
        in_specs=[
            pl.BlockSpec((tm, d), row),
            pl.BlockSpec(memory_space=pl.ANY),
            pl.BlockSpec((1, 1, d, tf), lambda i, k: (layer, k, 0, 0)),
            pl.BlockSpec((1, 1, d, tf), lambda i, k: (layer, nf + k, 0, 0)),
            pl.BlockSpec((1, tf, d), lambda i, k: (layer, k, 0)),
            _const_spec((1, d)),
        ],
        out_specs=pl.BlockSpec((tm, d), row),
        out_shape=jax.ShapeDtypeStruct((n, d), F32),
        scratch_shapes=[pltpu.VMEM((tm, d), F32), pltpu.SemaphoreType.DMA(())],
        compiler_params=_params(("arbitrary", "arbitrary"), VMEM_LIMIT_FFN),
        name="ffn_final" if final else "ffn",
    )(h2d, x2d, w_in, w_in, w_out, g_final)


def _mla_proj_body(x_ref, gm_ref, pos_ref, win_ref, gq_ref, gkv_ref, wqn_ref, wqr_ref, wk_ref, wv_ref,
                   freq_ref, sign_ref, q_ref, k_ref, v_ref):
    ts = x_ref.shape[0]
    rq = gq_ref.shape[1]
    rkv = gkv_ref.shape[1]
    heads = v_ref.shape[1] // V_HEAD_DIM
    half = QK_ROPE_DIM // 2
    h = _rms(x_ref[...], gm_ref[0]).astype(BF16)
    c = _dot(h, win_ref[...])
    cq = _rms(c[:, :rq], gq_ref[...]).astype(BF16)
    ckv = _rms(c[:, rq:rq + rkv], gkv_ref[...]).astype(BF16)

    ang = pos_ref[...] * freq_ref[...]
    cos = jnp.cos(ang)
    sin_signed = jnp.sin(ang) * sign_ref[...]
    lane = lax.broadcasted_iota(jnp.int32, (ts, LANES), 1)
    first_half = (lane & (QK_ROPE_DIM - 1)) < half
    low_lanes = lane < QK_ROPE_DIM

    def rope(x):
        swapped = jnp.where(first_half,
                            pltpu.roll(x, LANES - half, axis=1),
                            pltpu.roll(x, half, axis=1))
        return x * cos + swapped * sin_signed

    k_rope = rope(c[:, rq + rkv:]).astype(BF16)
    qr = _dot(cq, wqr_ref[...])
    for pair in range(heads // 2):
        r = rope(qr[:, pair * LANES:(pair + 1) * LANES])
        for j, part in enumerate((r, pltpu.roll(r, QK_ROPE_DIM, axis=1))):
            dst = (2 * pair + j) * QK_PAD_DIM + QK_NOPE_DIM
            q_ref[:, dst:dst + LANES] = jnp.where(low_lanes, part, 0.0).astype(BF16)
    group = 4
    for hc in range(heads // group):
        qn = _dot(cq, wqn_ref[:, hc * group * QK_NOPE_DIM:(hc + 1) * group * QK_NOPE_DIM])
        for j in range(group):
            dst = (hc * group + j) * QK_PAD_DIM
            q_ref[:, dst:dst + QK_NOPE_DIM] = qn[:, j * QK_NOPE_DIM:(j + 1) * QK_NOPE_DIM].astype(BF16)
    kn = _dot(ckv, wk_ref[...])
    for n in range(heads):
        dst = n * QK_PAD_DIM
        k_ref[:, dst:dst + QK_NOPE_DIM] = kn[:, n * QK_NOPE_DIM:(n + 1) * QK_NOPE_DIM].astype(BF16)
        k_ref[:, dst + QK_NOPE_DIM:dst + QK_PAD_DIM] = k_rope
    v_ref[...] = _dot(ckv, wv_ref[...]).astype(BF16)


def _mla_proj(x2d, gm, layer, pos, w_in, gq, gkv, wqn, wqr, wk, wv, freq, sign):
    n, d = x2d.shape
    ts = TS_PROJ
    heads = wv.shape[1] // V_HEAD_DIM
    qk_width = heads * QK_PAD_DIM
    row = lambda i: (i, 0)
    return pl.pallas_call(
        _mla_proj_body,
        grid=(n // ts,),
        in_specs=[
            pl.BlockSpec((ts, d), row),
            _layer_spec(gm, layer),
            pl.BlockSpec((ts, 1), row),
            _const_spec(w_in.shape),
            _const_spec(gq.shape),
            _const_spec(gkv.shape),
            _const_spec(wqn.shape),
            _const_spec(wqr.shape),
            _const_spec(wk.shape),
            _const_spec(wv.shape),
            _const_spec(freq.shape),
            _const_spec(sign.shape),
        ],
        out_specs=[
            pl.BlockSpec((ts, qk_width), row),
            pl.BlockSpec((ts, qk_width), row),
            pl.BlockSpec((ts, wv.shape[1]), row),
        ],
        out_shape=[
            jax.ShapeDtypeStruct((n, qk_width), BF16),
            jax.ShapeDtypeStruct((n, qk_width), BF16),
            jax.ShapeDtypeStruct((n, wv.shape[1]), BF16),
        ],
        compiler_params=_params(("arbitrary",)),
        name="mla_proj",
    )(x2d, gm, pos, w_in, gq, gkv, wqn, wqr, wk, wv, freq, sign)


def _attn_body(scale, q_ref, k_ref, v_ref, o_ref, vaug, s_buf, p_buf):
    seq = q_ref.shape[0]
    tq = TQ_ATTN
    nq = seq // tq
    dk = QK_PAD_DIM
    dv = V_HEAD_DIM
    heads = v_ref.shape[1] // dv
    c = scale * math.log2(math.e)
    row = lax.broadcasted_iota(jnp.int32, (tq, tq), 0)
    col = lax.broadcasted_iota(jnp.int32, (tq, tq), 1)
    causal = col <= row
    for h in range(heads):
        vaug[:, 2 * h * dv:(2 * h + 1) * dv] = v_ref[:, h * dv:(h + 1) * dv]
        vaug[:, (2 * h + 1) * dv:(2 * h + 2) * dv] = jnp.ones((seq, dv), BF16)

    tiles = []
    for h in range(heads):
        order = range(nq) if h % 2 == 0 else range(nq - 1, -1, -1)
        tiles += [(h, qi) for qi in order]

    scaled_row_max = {}

    def scores(t):
        h, qi = tiles[t]
        lo, nk, b = qi * tq, (qi + 1) * tq, t % 2
        s = _dot_nt(q_ref[lo:nk, h * dk:(h + 1) * dk], k_ref[0:nk, h * dk:(h + 1) * dk])
        s_diag = jnp.where(causal, s[:, lo:nk], MASK_VALUE)
        s_buf[b, :, lo:nk] = s_diag
        m_lanes = s_diag[:, 0:LANES]
        for j in range(LANES, tq, LANES):
            m_lanes = jnp.maximum(m_lanes, s_diag[:, j:j + LANES])
        if qi > 0:
            s_buf[b, :, 0:lo] = s[:, 0:lo]
            for j in range(0, lo, LANES):
                m_lanes = jnp.maximum(m_lanes, s[:, j:j + LANES])
        scaled_row_max[t] = jnp.max(m_lanes, axis=-1, keepdims=True) * c

    def softmax(t):
        _, qi = tiles[t]
        nk, b = (qi + 1) * tq, t % 2
        mc = scaled_row_max.pop(t)
        for j in range(0, nk, SOFTMAX_CHUNK):
            s = s_buf[b, :, j:j + SOFTMAX_CHUNK]
            p_buf[b, :, j:j + SOFTMAX_CHUNK] = jnp.exp2(s * c - mc).astype(BF16)

    def weighted_values(t):
        h, qi = tiles[t]
        lo, nk, b = qi * tq, (qi + 1) * tq, t % 2
        acc = _dot(p_buf[b, :, 0:nk], vaug[0:nk, 2 * h * dv:(2 * h + 2) * dv])
        o_ref[lo:nk, h * dv:(h + 1) * dv] = (acc[:, 0:dv] / acc[:, dv:2 * dv]).astype(BF16)

    scores(0)
    for t in range(len(tiles)):
        if t + 1 < len(tiles):
            scores(t + 1)
        softmax(t)
        weighted_values(t)


def _attention(q, k, v, batch, heads):
    n = q.shape[0]
    seq = n // batch
    hp = HEADS_PER_ATTN_STEP
    scale = 1.0 / math.sqrt(QK_NOPE_DIM + QK_ROPE_DIM)
    bh = lambda b, h: (b, h)
    return pl.pallas_call(
        functools.partial(_attn_body, scale),
        grid=(batch, heads // hp),
        in_specs=[
            pl.BlockSpec((seq, hp * QK_PAD_DIM), bh),
            pl.BlockSpec((seq, hp * QK_PAD_DIM), bh),
            pl.BlockSpec((seq, hp * V_HEAD_DIM), bh),
        ],
        out_specs=pl.BlockSpec((seq, hp * V_HEAD_DIM), bh),
        out_shape=jax.ShapeDtypeStruct((n, heads * V_HEAD_DIM), BF16),
        scratch_shapes=[
            pltpu.VMEM((seq, 2 * hp * V_HEAD_DIM), BF16),
            pltpu.VMEM((2, TQ_ATTN, seq), F32),
            pltpu.VMEM((2, TQ_ATTN, seq), BF16),
        ],
        compiler_params=_params(("arbitrary", "arbitrary")),
        name="mla_attention",
    )(q, k, v)


def _split_head_columns(w, head_dim, split):
    r = w.shape[0]
    heads = w.shape[1] // head_dim
    w = w.astype(BF16).reshape(r, heads, head_dim)
    a = w[:, :, :split].reshape(r, heads * split)
    b = w[:, :, split:].reshape(r, heads * (head_dim - split))
    return a, b


def _rope_rows():
    half = QK_ROPE_DIM // 2
    inv_freq = 1.0 / (ROPE_THETA ** (jnp.arange(0, QK_ROPE_DIM, 2, dtype=F32) / QK_ROPE_DIM))
    groups = LANES // QK_ROPE_DIM
    freq = jnp.tile(jnp.concatenate([inv_freq, inv_freq]), groups).reshape(1, LANES)
    sign = jnp.tile(jnp.concatenate([-jnp.ones((half,), F32), jnp.ones((half,), F32)]),
                    groups).reshape(1, LANES)
    return freq, sign


def kernel(x, mem, positions, norm_mix_g, norm_xattn_g, norm_mem_g, norm_ffn_g, pool_w, pool_scale, mla_w_in, mla_q_norm_g, mla_w_q_up, mla_kv_norm_g, mla_w_kv_up, mla_w_out, xattn_w_q, xattn_w_kv, xattn_w_o, ffn_w_in, ffn_w_out, final_norm_g):
    batch, seq, d = x.shape
    depth = norm_mix_g.shape[0]
    n_mixers = 2
    row = lambda g: g.reshape(1, -1)
    rows = lambda g: g.reshape(g.shape[0], 1, g.shape[1])

    gm, gx, gf = rows(norm_mix_g), rows(norm_xattn_g), rows(norm_ffn_g)
    wq_x, wo_x = xattn_w_q.astype(BF16), xattn_w_o.astype(BF16)
    w_ffn_in, w_ffn_out = _chunk_major(ffn_w_in.astype(BF16), TF_FFN), ffn_w_out.astype(BF16)
    w_pool, s_pool = pool_w.astype(BF16), rows(pool_scale)
    w_mla_out = mla_w_out.astype(BF16)

    k_mem, v_mem = _memkv(mem, norm_mem_g, xattn_w_kv.astype(BF16))
    freq, sign = _rope_rows()
    pos = positions.reshape(batch * seq, 1).astype(F32)

    x2d = x.reshape(batch * seq, d)
    for i in range(depth):
        j = i // n_mixers
        xattn_args = (gx, wq_x, k_mem, v_mem, wo_x, gf)
        if i % n_mixers == 0:
            x2d, h_ffn = _pool_xattn(x2d, batch, i, j, gm, w_pool, s_pool, *xattn_args)
        else:
            rq = mla_q_norm_g.shape[1]
            rkv = mla_kv_norm_g.shape[1]
            w_in = mla_w_in[j].astype(BF16)
            w_in = jnp.pad(w_in, ((0, 0), (0, rq + rkv + LANES - w_in.shape[1])))
            wqn, wqr = _split_head_columns(mla_w_q_up[j], QK_NOPE_DIM + QK_ROPE_DIM, QK_NOPE_DIM)
            wk, wv = _split_head_columns(mla_w_kv_up[j], QK_NOPE_DIM + V_HEAD_DIM, QK_NOPE_DIM)
            heads = wv.shape[1] // V_HEAD_DIM
            q, k, v = _mla_proj(x2d, gm, i, pos, w_in, row(mla_q_norm_g[j]), row(mla_kv_norm_g[j]),
                                wqn, wqr, wk, wv, freq, sign)
            o = _attention(q, k, v, batch, heads)
            x2d, h_ffn = _oproj_xattn(x2d, o, batch, i, j, w_mla_out, *xattn_args)
        x2d = _ffn(h_ffn, x2d, i, w_ffn_in, w_ffn_out, row(final_norm_g), i == depth - 1)
    return x2d.reshape(batch, seq, d)
```

```python
import functools
import math

import jax
import jax.numpy as jnp
from jax import lax
from jax.experimental import pallas as pl
from jax.experimental.pallas import tpu as pltpu

F32 = jnp.float32
BF16 = jnp.bfloat16

NORM_EPS = 1e-6
MASK_VALUE = -1e30
ROPE_THETA = 10000.0
POOL_WINDOWS = (2, 4, 8, 16)
QK_NOPE_DIM = 128
QK_ROPE_DIM = 64
V_HEAD_DIM = 128
XATTN_HEADS = 4

LANES = 128
QK_PAD_DIM = 256
POOL_HALO = 32

TS_MIX = 512
TS_PROJ = 512
TQ_ATTN = 256
HEADS_PER_ATTN_STEP = 2
SOFTMAX_CHUNK = 256
TM_FFN = 1024
TF_FFN = 512

VMEM_LIMIT = 48 * 1024 * 1024
VMEM_LIMIT_FFN = 60 * 1024 * 1024


def _rms(x, g):
    var = jnp.mean(x * x, axis=-1, keepdims=True)
    return x * lax.rsqrt(var + NORM_EPS) * g


def _dot(a, b):
    return jnp.dot(a, b, preferred_element_type=F32)


def _dot_nt(a, b):
    return lax.dot_general(a, b, (((1,), (1,)), ((), ())), preferred_element_type=F32)


def _const_spec(shape):
    nd = len(shape)
    return pl.BlockSpec(shape, lambda *_: (0,) * nd)


def _layer_spec(stacked, layer):
    tail = stacked.shape[1:]
    return pl.BlockSpec((1,) + tail, lambda *_: (layer,) + (0,) * len(tail))


def _params(sem, vmem_limit=VMEM_LIMIT):
    return pltpu.CompilerParams(dimension_semantics=sem, vmem_limit_bytes=vmem_limit)


def _memkv_body(mem_ref, g_ref, w_ref, k_ref, v_ref):
    xw = k_ref.shape[-1]
    hd = xw // XATTN_HEADS
    mn = _rms(mem_ref[0], g_ref[0]).astype(BF16)
    kv = _dot(mn, w_ref[0])
    k_ref[0, 0] = kv[:, :xw].astype(BF16)
    ones = jnp.ones((kv.shape[0], hd), BF16)
    for n in range(XATTN_HEADS):
        v_ref[0, 0, :, 2 * n * hd:(2 * n + 1) * hd] = kv[:, xw + n * hd:xw + (n + 1) * hd].astype(BF16)
        v_ref[0, 0, :, (2 * n + 1) * hd:(2 * n + 2) * hd] = ones


def _memkv(mem, g, w_kv):
    depth, d, xw2 = w_kv.shape
    b, m, _ = mem.shape
    xw = xw2 // 2
    return pl.pallas_call(
        _memkv_body,
        grid=(depth, b),
        in_specs=[
            pl.BlockSpec((1, m, d), lambda l, i: (i, 0, 0)),
            pl.BlockSpec((1, 1, d), lambda l, i: (l, 0, 0)),
            pl.BlockSpec((1, d, xw2), lambda l, i: (l, 0, 0)),
        ],
        out_specs=[
            pl.BlockSpec((1, 1, m, xw), lambda l, i: (l, i, 0, 0)),
            pl.BlockSpec((1, 1, m, xw2), lambda l, i: (l, i, 0, 0)),
        ],
        out_shape=[jax.ShapeDtypeStruct((depth, b, m, xw), BF16),
                   jax.ShapeDtypeStruct((depth, b, m, xw2), BF16)],
        compiler_params=_params(("arbitrary", "arbitrary")),
        name="memkv",
    )(mem, g.reshape(depth, 1, d), w_kv)


def _xattn_tail(x1, gx_ref, wq_ref, k_ref, v_ref, wo_ref, gf_ref, x2_ref, h3_ref):
    xw = wq_ref.shape[-1]
    hd = xw // XATTN_HEADS
    c = math.log2(math.e) / math.sqrt(hd)
    h2 = _rms(x1, gx_ref[0]).astype(BF16)
    q = _dot(h2, wq_ref[0]).astype(BF16)
    heads = []
    for n in range(XATTN_HEADS):
        sl = slice(n * hd, (n + 1) * hd)
        s = _dot_nt(q[:, sl], k_ref[0, 0, :, sl])
        mc = jnp.max(s, axis=-1, keepdims=True) * c
        p = jnp.exp2(s * c - mc).astype(BF16)
        acc = _dot(p, v_ref[0, 0, :, 2 * n * hd:(2 * n + 2) * hd])
        heads.append((acc[:, :hd] / acc[:, hd:]).astype(BF16))
    o = jnp.concatenate(heads, axis=-1)
    x2 = x1 + _dot(o, wo_ref[0])
    x2_ref[...] = x2
    h3_ref[...] = _rms(x2, gf_ref[0]).astype(BF16)


def _pool_xattn_body(x_ref, halo_ref, gm_ref, pw_ref, ps_ref,
                     gx_ref, wq_ref, k_ref, v_ref, wo_ref, gf_ref,
                     x2_ref, h3_ref, buf_a, buf_b):
    ts, d = x_ref.shape
    dg = d // len(POOL_WINDOWS)
    rows = ts + POOL_HALO
    i = pl.program_id(1)
    x = x_ref[...]
    hx = _rms(x, gm_ref[0])
    hh = _rms(halo_ref[...], gm_ref[0]) * (i > 0).astype(F32)
    t = i * ts + lax.broadcasted_iota(jnp.int32, (ts, 1), 0)
    x1 = []
    for g, w in enumerate(POOL_WINDOWS):
        sl = slice(g * dg, (g + 1) * dg)
        buf_a[0:POOL_HALO, :] = hh[:, sl]
        buf_a[POOL_HALO:rows, :] = hx[:, sl]
        src, dst, lo, shift = buf_a, buf_b, 0, w // 2
        while shift > 1:
            lo += 8
            dst[lo:rows, :] = src[lo:rows, :] + src[lo - shift:rows - shift, :]
            src, dst, shift = dst, src, shift // 2
        wsum = src[POOL_HALO:rows, :] + src[POOL_HALO - 1:rows - 1, :]
        inv_cnt = 1.0 / jnp.minimum(t + 1, w).astype(F32)
        pooled = (wsum * inv_cnt - hx[:, sl]).astype(BF16)
        x1.append(x[:, sl] + _dot(pooled, pw_ref[0, g]) * ps_ref[0, :, sl])
    _xattn_tail(jnp.concatenate(x1, axis=-1),
                gx_ref, wq_ref, k_ref, v_ref, wo_ref, gf_ref, x2_ref, h3_ref)


def _pool_xattn(x2d, batch, layer, pool_layer, gm, pool_w, pool_scale, gx, wq, k_mem, v_mem, wo, gf):
    n, d = x2d.shape
    seq = n // batch
    ts = TS_MIX
    nt = seq // ts
    hb = ts // POOL_HALO
    dg = pool_w.shape[-1]
    _, _, m, xw = k_mem.shape
    row = lambda b, i: (b * nt + i, 0)
    halo = lambda b, i: (jnp.maximum((b * nt + i) * hb - 1, 0), 0)
    memkv = lambda b, i: (layer, b, 0, 0)
    return pl.pallas_call(
        _pool_xattn_body,
        grid=(batch, nt),
        in_specs=[
            pl.BlockSpec((ts, d), row),
            pl.BlockSpec((POOL_HALO, d), halo),
            _layer_spec(gm, layer),
            _layer_spec(pool_w, pool_layer),
            _layer_spec(pool_scale, pool_layer),
            _layer_spec(gx, layer),
            _layer_spec(wq, layer),
            pl.BlockSpec((1, 1, m, xw), memkv),
            pl.BlockSpec((1, 1, m, 2 * xw), memkv),
            _layer_spec(wo, layer),
            _layer_spec(gf, layer),
        ],
        out_specs=[pl.BlockSpec((ts, d), row), pl.BlockSpec((ts, d), row)],
        out_shape=[jax.ShapeDtypeStruct((n, d), F32), jax.ShapeDtypeStruct((n, d), BF16)],
        scratch_shapes=[
            pltpu.VMEM((ts + POOL_HALO, dg), F32),
            pltpu.VMEM((ts + POOL_HALO, dg), F32),
        ],
        compiler_params=_params(("arbitrary", "arbitrary")),
        name="pool_xattn",
    )(x2d, x2d, gm, pool_w, pool_scale, gx, wq, k_mem, v_mem, wo, gf)


def _oproj_xattn_body(x_ref, o_ref, wout_ref,
                      gx_ref, wq_ref, k_ref, v_ref, wo_ref, gf_ref, x2_ref, h3_ref):
    x1 = x_ref[...] + _dot(o_ref[...], wout_ref[0])
    _xattn_tail(x1, gx_ref, wq_ref, k_ref, v_ref, wo_ref, gf_ref, x2_ref, h3_ref)


def _oproj_xattn(x2d, o2d, batch, layer, mla_layer, w_out, gx, wq, k_mem, v_mem, wo, gf):
    n, d = x2d.shape
    seq = n // batch
    ts = TS_MIX
    nt = seq // ts
    _, _, m, xw = k_mem.shape
    row = lambda b, i: (b * nt + i, 0)
    memkv = lambda b, i: (layer, b, 0, 0)
    return pl.pallas_call(
        _oproj_xattn_body,
        grid=(batch, nt),
        in_specs=[
            pl.BlockSpec((ts, d), row),
            pl.BlockSpec((ts, o2d.shape[1]), row),
            _layer_spec(w_out, mla_layer),
            _layer_spec(gx, layer),
            _layer_spec(wq, layer),
            pl.BlockSpec((1, 1, m, xw), memkv),
            pl.BlockSpec((1, 1, m, 2 * xw), memkv),
            _layer_spec(wo, layer),
            _layer_spec(gf, layer),
        ],
        out_specs=[pl.BlockSpec((ts, d), row), pl.BlockSpec((ts, d), row)],
        out_shape=[jax.ShapeDtypeStruct((n, d), F32), jax.ShapeDtypeStruct((n, d), BF16)],
        compiler_params=_params(("arbitrary", "arbitrary")),
        name="oproj_xattn",
    )(x2d, o2d, w_out, gx, wq, k_mem, v_mem, wo, gf)


def _ffn_body(final, prefetch_step, h_ref, x_hbm, wg_ref, wu_ref, wo_ref, g_ref, acc_ref, xbuf, sem):
    i = pl.program_id(0)
    k = pl.program_id(1)
    tm = acc_ref.shape[0]

    def x_copy(tile):
        start = tile * tm
        if not isinstance(start, int):
            start = pl.multiple_of(start, tm)
        return pltpu.make_async_copy(x_hbm.at[pl.ds(start, tm), :], xbuf, sem)

    @pl.when(jnp.logical_and(i == 0, k == 0))
    def _():
        x_copy(0).start()

    @pl.when(k == 0)
    def _():
        x_copy(i).wait()
        acc_ref[...] = xbuf[...]

    @pl.when(jnp.logical_and(k == prefetch_step, i + 1 < pl.num_programs(0)))
    def _():
        x_copy(i + 1).start()

    h = h_ref[...]
    gate = _dot(h, wg_ref[0])
    up = _dot(h, wu_ref[0])
    act = (gate / (1.0 + jnp.exp(-gate)) * up).astype(BF16)
    acc_ref[...] += _dot(act, wo_ref[0].astype(BF16))

    if final:
        @pl.when(k == pl.num_programs(1) - 1)
        def _():
            acc_ref[...] = _rms(acc_ref[...], g_ref[...])


def _ffn(h2d, x2d, layer, w_in, w_out, g_final, final):
    n, d = x2d.shape
    f = w_out.shape[1]
    tm, tf = TM_FFN, TF_FFN
    nf = f // tf
    assert nf >= 2 and n % tm == 0 and f % tf == 0
    row = lambda i, k: (i, 0)
    return pl.pallas_call(
        functools.partial(_ffn_body, final, nf // 2),
        grid=(n // tm, nf),
        in_specs=[
            pl.BlockSpec((tm, d), row),
            pl.BlockSpec(memory_space=pl.ANY),
            pl.BlockSpec((1, d, tf), lambda i, k: (layer, 0, k)),
            pl.BlockSpec((1, d, tf), lambda i, k: (layer, 0, nf + k)),
            pl.BlockSpec((1, tf, d), lambda i, k: (layer, k, 0)),
            _const_spec((1, d)),
        ],
        out_specs=pl.BlockSpec((tm, d), row),
        out_shape=jax.ShapeDtypeStruct((n, d), F32),
        scratch_shapes=[pltpu.VMEM((tm, d), F32), pltpu.SemaphoreType.DMA(())],
        compiler_params=_params(("arbitrary", "arbitrary"), VMEM_LIMIT_FFN),
        name="ffn_final" if final else "ffn",
    )(h2d, x2d, w_in, w_in, w_out, g_final)


def _mla_proj_body(x_ref, gm_ref, pos_ref, win_ref, gq_ref, gkv_ref, wqn_ref, wqr_ref, wk_ref, wv_ref,
                   freq_ref, sign_ref, q_ref, k_ref, v_ref):
    ts = x_ref.shape[0]
    rq = gq_ref.shape[1]
    rkv = gkv_ref.shape[1]
    heads = v_ref.shape[1] // V_HEAD_DIM
    half = QK_ROPE_DIM // 2
    h = _rms(x_ref[...], gm_ref[0]).astype(BF16)
    c = _dot(h, win_ref[...])
    cq = _rms(c[:, :rq], gq_ref[...]).astype(BF16)
    ckv = _rms(c[:, rq:rq + rkv], gkv_ref[...]).astype(BF16)

    ang = pos_ref[...] * freq_ref[...]
    cos = jnp.cos(ang)
    sin_signed = jnp.sin(ang) * sign_ref[...]
    lane = lax.broadcasted_iota(jnp.int32, (ts, LANES), 1)
    first_half = (lane & (QK_ROPE_DIM - 1)) < half
    low_lanes = lane < QK_ROPE_DIM

    def rope(x):
        swapped = jnp.where(first_half,
                            pltpu.roll(x, LANES - half, axis=1),
                            pltpu.roll(x, half, axis=1))
        return x * cos + swapped * sin_signed

    k_rope = rope(c[:, rq + rkv:]).astype(BF16)
    qr = _dot(cq, wqr_ref[...])
    for pair in range(heads // 2):
        r = rope(qr[:, pair * LANES:(pair + 1) * LANES])
        for j, part in enumerate((r, pltpu.roll(r, QK_ROPE_DIM, axis=1))):
            dst = (2 * pair + j) * QK_PAD_DIM + QK_NOPE_DIM
            q_ref[:, dst:dst + LANES] = jnp.where(low_lanes, part, 0.0).astype(BF16)
    group = 4
    for hc in range(heads // group):
        qn = _dot(cq, wqn_ref[:, hc * group * QK_NOPE_DIM:(hc + 1) * group * QK_NOPE_DIM])
        for j in range(group):
            dst = (hc * group + j) * QK_PAD_DIM
            q_ref[:, dst:dst + QK_NOPE_DIM] = qn[:, j * QK_NOPE_DIM:(j + 1) * QK_NOPE_DIM].astype(BF16)
    kn = _dot(ckv, wk_ref[...])
    for n in range(heads):
        dst = n * QK_PAD_DIM
        k_ref[:, dst:dst + QK_NOPE_DIM] = kn[:, n * QK_NOPE_DIM:(n + 1) * QK_NOPE_DIM].astype(BF16)
        k_ref[:, dst + QK_NOPE_DIM:dst + QK_PAD_DIM] = k_rope
    v_ref[...] = _dot(ckv, wv_ref[...]).astype(BF16)


def _mla_proj(x2d, gm, layer, pos, w_in, gq, gkv, wqn, wqr, wk, wv, freq, sign):
    n, d = x2d.shape
    ts = TS_PROJ
    heads = wv.shape[1] // V_HEAD_DIM
    qk_width = heads * QK_PAD_DIM
    row = lambda i: (i, 0)
    return pl.pallas_call(
        _mla_proj_body,
        grid=(n // ts,),
        in_specs=[
            pl.BlockSpec((ts, d), row),
            _layer_spec(gm, layer),
            pl.BlockSpec((ts, 1), row),
            _const_spec(w_in.shape),
            _const_spec(gq.shape),
            _const_spec(gkv.shape),
            _const_spec(wqn.shape),
            _const_spec(wqr.shape),
            _const_spec(wk.shape),
            _const_spec(wv.shape),
            _const_spec(freq.shape),
            _const_spec(sign.shape),
        ],
        out_specs=[
            pl.BlockSpec((ts, qk_width), row),
            pl.BlockSpec((ts, qk_width), row),
            pl.BlockSpec((ts, wv.shape[1]), row),
        ],
        out_shape=[
            jax.ShapeDtypeStruct((n, qk_width), BF16),
            jax.ShapeDtypeStruct((n, qk_width), BF16),
            jax.ShapeDtypeStruct((n, wv.shape[1]), BF16),
        ],
        compiler_params=_params(("arbitrary",)),
        name="mla_proj",
    )(x2d, gm, pos, w_in, gq, gkv, wqn, wqr, wk, wv, freq, sign)


def _attn_body(scale, q_ref, k_ref, v_ref, o_ref, vaug, s_buf, p_buf):
    seq = q_ref.shape[0]
    tq = TQ_ATTN
    nq = seq // tq
    dk = QK_PAD_DIM
    dv = V_HEAD_DIM
    heads = v_ref.shape[1] // dv
    c = scale * math.log2(math.e)
    row = lax.broadcasted_iota(jnp.int32, (tq, tq), 0)
    col = lax.broadcasted_iota(jnp.int32, (tq, tq), 1)
    causal = col <= row
    for h in range(heads):
        vaug[:, 2 * h * dv:(2 * h + 1) * dv] = v_ref[:, h * dv:(h + 1) * dv]
        vaug[:, (2 * h + 1) * dv:(2 * h + 2) * dv] = jnp.ones((seq, dv), BF16)

    tiles = []
    for h in range(heads):
        order = range(nq) if h % 2 == 0 else range(nq - 1, -1, -1)
        tiles += [(h, qi) for qi in order]

    scaled_row_max = {}

    def scores(t):
        h, qi = tiles[t]
        lo, nk, b = qi * tq, (qi + 1) * tq, t % 2
        s = _dot_nt(q_ref[lo:nk, h * dk:(h + 1) * dk], k_ref[0:nk, h * dk:(h + 1) * dk])
        s_diag = jnp.where(causal, s[:, lo:nk], MASK_VALUE)
        s_buf[b, :, lo:nk] = s_diag
        m_lanes = s_diag[:, 0:LANES]
        for j in range(LANES, tq, LANES):
            m_lanes = jnp.maximum(m_lanes, s_diag[:, j:j + LANES])
        if qi > 0:
            s_buf[b, :, 0:lo] = s[:, 0:lo]
            for j in range(0, lo, LANES):
                m_lanes = jnp.maximum(m_lanes, s[:, j:j + LANES])
        scaled_row_max[t] = jnp.max(m_lanes, axis=-1, keepdims=True) * c

    def softmax(t):
        _, qi = tiles[t]
        nk, b = (qi + 1) * tq, t % 2
        mc = scaled_row_max.pop(t)
        for j in range(0, nk, SOFTMAX_CHUNK):
            s = s_buf[b, :, j:j + SOFTMAX_CHUNK]
            p_buf[b, :, j:j + SOFTMAX_CHUNK] = jnp.exp2(s * c - mc).astype(BF16)

    def weighted_values(t):
        h, qi = tiles[t]
        lo, nk, b = qi * tq, (qi + 1) * tq, t % 2
        acc = _dot(p_buf[b, :, 0:nk], vaug[0:nk, 2 * h * dv:(2 * h + 2) * dv])
        o_ref[lo:nk, h * dv:(h + 1) * dv] = (acc[:, 0:dv] / acc[:, dv:2 * dv]).astype(BF16)

    scores(0)
    for t in range(len(tiles)):
        if t + 1 < len(tiles):
            scores(t + 1)
        softmax(t)
        weighted_values(t)


def _attention(q, k, v, batch, heads):
    n = q.shape[0]
    seq = n // batch
    hp = HEADS_PER_ATTN_STEP
    scale = 1.0 / math.sqrt(QK_NOPE_DIM + QK_ROPE_DIM)
    bh = lambda b, h: (b, h)
    return pl.pallas_call(
        functools.partial(_attn_body, scale),
        grid=(batch, heads // hp),
        in_specs=[
            pl.BlockSpec((seq, hp * QK_PAD_DIM), bh),
            pl.BlockSpec((seq, hp * QK_PAD_DIM), bh),
            pl.BlockSpec((seq, hp * V_HEAD_DIM), bh),
        ],
        out_specs=pl.BlockSpec((seq, hp * V_HEAD_DIM), bh),
        out_shape=jax.ShapeDtypeStruct((n, heads * V_HEAD_DIM), BF16),
        scratch_shapes=[
            pltpu.VMEM((seq, 2 * hp * V_HEAD_DIM), BF16),
            pltpu.VMEM((2, TQ_ATTN, seq), F32),
            pltpu.VMEM((2, TQ_ATTN, seq), BF16),
        ],
        compiler_params=_params(("arbitrary", "arbitrary")),
        name="mla_attention",
    )(q, k, v)


def _split_head_columns(w, head_dim, split):
    r = w.shape[0]
    heads = w.shape[1] // head_dim
    w = w.astype(BF16).reshape(r, heads, head_dim)
    a = w[:, :, :split].reshape(r, heads * split)
    b = w[:, :, split:].reshape(r, heads * (head_dim - split))
    return a, b


def _rope_rows():
    half = QK_ROPE_DIM // 2
    inv_freq = 1.0 / (ROPE_THETA ** (jnp.arange(0, QK_ROPE_DIM, 2, dtype=F32) / QK_ROPE_DIM))
    groups = LANES // QK_ROPE_DIM
    freq = jnp.tile(jnp.concatenate([inv_freq, inv_freq]), groups).reshape(1, LANES)
    sign = jnp.tile(jnp.concatenate([-jnp.ones((half,), F32), jnp.ones((half,), F32)]),
                    groups).reshape(1, LANES)
    return freq, sign


def kernel(x, mem, positions, norm_mix_g, norm_xattn_g, norm_mem_g, norm_ffn_g, pool_w, pool_scale, mla_w_in, mla_q_norm_g, mla_w_q_up, mla_kv_norm_g, mla_w_kv_up, mla_w_out, xattn_w_q, xattn_w_kv, xattn_w_o, ffn_w_in, ffn_w_out, final_norm_g):
    batch, seq, d = x.shape
    depth = norm_mix_g.shape[0]
    n_mixers = 2
    row = lambda g: g.reshape(1, -1)
    rows = lambda g: g.reshape(g.shape[0], 1, g.shape[1])

    gm, gx, gf = rows(norm_mix_g), rows(norm_xattn_g), rows(norm_ffn_g)
    wq_x, wo_x = xattn_w_q.astype(BF16), xattn_w_o.astype(BF16)
    w_ffn_in, w_ffn_out = ffn_w_in.astype(BF16), ffn_w_out
    w_pool, s_pool = pool_w.astype(BF16), rows(pool_scale)
    w_mla_out = mla_w_out.astype(BF16)

    k_mem, v_mem = _memkv(mem, norm_mem_g, xattn_w_kv.astype(BF16))
    freq, sign = _rope_rows()
    pos = positions.reshape(batch * seq, 1).astype(F32)

    x2d = x.reshape(batch * seq, d)
    for i in range(depth):
        j = i // n_mixers
        xattn_args = (gx, wq_x, k_mem, v_mem, wo_x, gf)
        if i % n_mixers == 0:
            x2d, h_ffn = _pool_xattn(x2d, batch, i, j, gm, w_pool, s_pool, *xattn_args)
        else:
            rq = mla_q_norm_g.shape[1]
            rkv = mla_kv_norm_g.shape[1]
            w_in = mla_w_in[j].astype(BF16)
            w_in = jnp.pad(w_in, ((0, 0), (0, rq + rkv + LANES - w_in.shape[1])))
            wqn, wqr = _split_head_columns(mla_w_q_up[j], QK_NOPE_DIM + QK_ROPE_DIM, QK_NOPE_DIM)
            wk, wv = _split_head_columns(mla_w_kv_up[j], QK_NOPE_DIM + V_HEAD_DIM, QK_NOPE_DIM)
            heads = wv.shape[1] // V_HEAD_DIM
            q, k, v = _mla_proj(x2d, gm, i, pos, w_in, row(mla_q_norm_g[j]), row(mla_kv_norm_g[j]),
                                wqn, wqr, wk, wv, freq, sign)
            o = _attention(q, k, v, batch, heads)
            x2d, h_ffn = _oproj_xattn(x2d, o, batch, i, j, w_mla_out, *xattn_args)
        x2d = _ffn(h_ffn, x2d, i, w_ffn_in, w_ffn_out, row(final_norm_g), i == depth - 1)
    return x2d.reshape(batch, seq, d)
```

```python
import functools
import math

import jax
import jax.numpy as jnp
from jax import lax
from jax.experimental import pallas as pl
from jax.experimental.pallas import tpu as pltpu

F32 = jnp.float32
BF16 = jnp.bfloat16

NORM_EPS = 1e-6
MASK_VALUE = -1e30
ROPE_THETA = 10000.0
POOL_WINDOWS = (2, 4, 8, 16)
QK_NOPE_DIM = 128
QK_ROPE_DIM = 64
V_HEAD_DIM = 128
XATTN_HEADS = 4

LANES = 128
QK_PAD_DIM = 256
POOL_HALO = 32

TS_MIX = 512
TS_PROJ = 512
TQ_ATTN = 256
HEADS_PER_ATTN_STEP = 2
SOFTMAX_CHUNK = 256
TM_FFN = 1024
TF_FFN = 512

VMEM_LIMIT = 48 * 1024 * 1024
VMEM_LIMIT_FFN = 62 * 1024 * 1024


def _rms(x, g):
    var = jnp.mean(x * x, axis=-1, keepdims=True)
    return x * lax.rsqrt(var + NORM_EPS) * g


def _dot(a, b):
    return jnp.dot(a, b, preferred_element_type=F32)


def _dot_nt(a, b):
    return lax.dot_general(a, b, (((1,), (1,)), ((), ())), preferred_element_type=F32)


def _const_spec(shape):
    nd = len(shape)
    return pl.BlockSpec(shape, lambda *_: (0,) * nd)


def _layer_spec(stacked, layer):
    tail = stacked.shape[1:]
    return pl.BlockSpec((1,) + tail, lambda *_: (layer,) + (0,) * len(tail))


def _params(sem, vmem_limit=VMEM_LIMIT):
    return pltpu.CompilerParams(dimension_semantics=sem, vmem_limit_bytes=vmem_limit)


def _memkv_body(mem_ref, g_ref, w_ref, k_ref, v_ref):
    xw = k_ref.shape[-1]
    hd = xw // XATTN_HEADS
    mn = _rms(mem_ref[0], g_ref[0]).astype(BF16)
    kv = _dot(mn, w_ref[0])
    k_ref[0, 0] = kv[:, :xw].astype(BF16)
    ones = jnp.ones((kv.shape[0], hd), BF16)
    for n in range(XATTN_HEADS):
        v_ref[0, 0, :, 2 * n * hd:(2 * n + 1) * hd] = kv[:, xw + n * hd:xw + (n + 1) * hd].astype(BF16)
        v_ref[0, 0, :, (2 * n + 1) * hd:(2 * n + 2) * hd] = ones


def _memkv(mem, g, w_kv):
    depth, d, xw2 = w_kv.shape
    b, m, _ = mem.shape
    xw = xw2 // 2
    return pl.pallas_call(
        _memkv_body,
        grid=(depth, b),
        in_specs=[
            pl.BlockSpec((1, m, d), lambda l, i: (i, 0, 0)),
            pl.BlockSpec((1, 1, d), lambda l, i: (l, 0, 0)),
            pl.BlockSpec((1, d, xw2), lambda l, i: (l, 0, 0)),
        ],
        out_specs=[
            pl.BlockSpec((1, 1, m, xw), lambda l, i: (l, i, 0, 0)),
            pl.BlockSpec((1, 1, m, xw2), lambda l, i: (l, i, 0, 0)),
        ],
        out_shape=[jax.ShapeDtypeStruct((depth, b, m, xw), BF16),
                   jax.ShapeDtypeStruct((depth, b, m, xw2), BF16)],
        compiler_params=_params(("arbitrary", "arbitrary")),
        name="memkv",
    )(mem, g.reshape(depth, 1, d), w_kv)


def _xattn_tail(x1, gx_ref, wq_ref, k_ref, v_ref, wo_ref, gf_ref, x2_ref, h3_ref):
    xw = wq_ref.shape[-1]
    hd = xw // XATTN_HEADS
    c = math.log2(math.e) / math.sqrt(hd)
    h2 = _rms(x1, gx_ref[0]).astype(BF16)
    q = _dot(h2, wq_ref[0]).astype(BF16)
    heads = []
    for n in range(XATTN_HEADS):
        sl = slice(n * hd, (n + 1) * hd)
        s = _dot_nt(q[:, sl], k_ref[0, 0, :, sl])
        mc = jnp.max(s, axis=-1, keepdims=True) * c
        p = jnp.exp2(s * c - mc).astype(BF16)
        acc = _dot(p, v_ref[0, 0, :, 2 * n * hd:(2 * n + 2) * hd])
        heads.append((acc[:, :hd] / acc[:, hd:]).astype(BF16))
    o = jnp.concatenate(heads, axis=-1)
    x2 = x1 + _dot(o, wo_ref[0])
    x2_ref[...] = x2
    h3_ref[...] = _rms(x2, gf_ref[0]).astype(BF16)


def _pool_xattn_body(x_ref, halo_ref, gm_ref, pw_ref, ps_ref,
                     gx_ref, wq_ref, k_ref, v_ref, wo_ref, gf_ref,
                     x2_ref, h3_ref, buf_a, buf_b):
    ts, d = x_ref.shape
    dg = d // len(POOL_WINDOWS)
    rows = ts + POOL_HALO
    i = pl.program_id(1)
    x = x_ref[...]
    hx = _rms(x, gm_ref[0])
    hh = _rms(halo_ref[...], gm_ref[0]) * (i > 0).astype(F32)
    t = i * ts + lax.broadcasted_iota(jnp.int32, (ts, 1), 0)
    x1 = []
    for g, w in enumerate(POOL_WINDOWS):
        sl = slice(g * dg, (g + 1) * dg)
        buf_a[0:POOL_HALO, :] = hh[:, sl]
        buf_a[POOL_HALO:rows, :] = hx[:, sl]
        src, dst, lo, shift = buf_a, buf_b, 0, w // 2
        while shift > 1:
            lo += 8
            dst[lo:rows, :] = src[lo:rows, :] + src[lo - shift:rows - shift, :]
            src, dst, shift = dst, src, shift // 2
        wsum = src[POOL_HALO:rows, :] + src[POOL_HALO - 1:rows - 1, :]
        inv_cnt = 1.0 / jnp.minimum(t + 1, w).astype(F32)
        pooled = (wsum * inv_cnt - hx[:, sl]).astype(BF16)
        x1.append(x[:, sl] + _dot(pooled, pw_ref[0, g]) * ps_ref[0, :, sl])
    _xattn_tail(jnp.concatenate(x1, axis=-1),
                gx_ref, wq_ref, k_ref, v_ref, wo_ref, gf_ref, x2_ref, h3_ref)


def _pool_xattn(x2d, batch, layer, pool_layer, gm, pool_w, pool_scale, gx, wq, k_mem, v_mem, wo, gf):
    n, d = x2d.shape
    seq = n // batch
    ts = TS_MIX
    nt = seq // ts
    hb = ts // POOL_HALO
    dg = pool_w.shape[-1]
    _, _, m, xw = k_mem.shape
    row = lambda b, i: (b * nt + i, 0)
    halo = lambda b, i: (jnp.maximum((b * nt + i) * hb - 1, 0), 0)
    memkv = lambda b, i: (layer, b, 0, 0)
    return pl.pallas_call(
        _pool_xattn_body,
        grid=(batch, nt),
        in_specs=[
            pl.BlockSpec((ts, d), row),
            pl.BlockSpec((POOL_HALO, d), halo),
            _layer_spec(gm, layer),
            _layer_spec(pool_w, pool_layer),
            _layer_spec(pool_scale, pool_layer),
            _layer_spec(gx, layer),
            _layer_spec(wq, layer),
            pl.BlockSpec((1, 1, m, xw), memkv),
            pl.BlockSpec((1, 1, m, 2 * xw), memkv),
            _layer_spec(wo, layer),
            _layer_spec(gf, layer),
        ],
        out_specs=[pl.BlockSpec((ts, d), row), pl.BlockSpec((ts, d), row)],
        out_shape=[jax.ShapeDtypeStruct((n, d), F32), jax.ShapeDtypeStruct((n, d), BF16)],
        scratch_shapes=[
            pltpu.VMEM((ts + POOL_HALO, dg), F32),
            pltpu.VMEM((ts + POOL_HALO, dg), F32),
        ],
        compiler_params=_params(("arbitrary", "arbitrary")),
        name="pool_xattn",
    )(x2d, x2d, gm, pool_w, pool_scale, gx, wq, k_mem, v_mem, wo, gf)


def _oproj_xattn_body(x_ref, o_ref, wout_ref,
                      gx_ref, wq_ref, k_ref, v_ref, wo_ref, gf_ref, x2_ref, h3_ref):
    x1 = x_ref[...] + _dot(o_ref[...], wout_ref[0])
    _xattn_tail(x1, gx_ref, wq_ref, k_ref, v_ref, wo_ref, gf_ref, x2_ref, h3_ref)


def _oproj_xattn(x2d, o2d, batch, layer, mla_layer, w_out, gx, wq, k_mem, v_mem, wo, gf):
    n, d = x2d.shape
    seq = n // batch
    ts = TS_MIX
    nt = seq // ts
    _, _, m, xw = k_mem.shape
    row = lambda b, i: (b * nt + i, 0)
    memkv = lambda b, i: (layer, b, 0, 0)
    return pl.pallas_call(
        _oproj_xattn_body,
        grid=(batch, nt),
        in_specs=[
            pl.BlockSpec((ts, d), row),
            pl.BlockSpec((ts, o2d.shape[1]), row),
            _layer_spec(w_out, mla_layer),
            _layer_spec(gx, layer),
            _layer_spec(wq, layer),
            pl.BlockSpec((1, 1, m, xw), memkv),
            pl.BlockSpec((1, 1, m, 2 * xw), memkv),
            _layer_spec(wo, layer),
            _layer_spec(gf, layer),
        ],
        out_specs=[pl.BlockSpec((ts, d), row), pl.BlockSpec((ts, d), row)],
        out_shape=[jax.ShapeDtypeStruct((n, d), F32), jax.ShapeDtypeStruct((n, d), BF16)],
        compiler_params=_params(("arbitrary", "arbitrary")),
        name="oproj_xattn",
    )(x2d, o2d, w_out, gx, wq, k_mem, v_mem, wo, gf)


def _ffn_body(final, prefetch_step, h_ref, x_hbm, wg_ref, wu_ref, wo_ref, g_ref, acc_ref, xbuf, sem):
    i = pl.program_id(0)
    k = pl.program_id(1)
    tm = acc_ref.shape[0]

    def x_copy(tile):
        start = tile * tm
        if not isinstance(start, int):
            start = pl.multiple_of(start, tm)
        return pltpu.make_async_copy(x_hbm.at[pl.ds(start, tm), :], xbuf, sem)

    @pl.when(jnp.logical_and(i == 0, k == 0))
    def _():
        x_copy(0).start()

    @pl.when(k == 0)
    def _():
        x_copy(i).wait()
        acc_ref[...] = xbuf[...]

    @pl.when(jnp.logical_and(k == prefetch_step, i + 1 < pl.num_programs(0)))
    def _():
        x_copy(i + 1).start()

    h = h_ref[...]
    gate = _dot(h, wg_ref[0].astype(BF16))
    up = _dot(h, wu_ref[0])
    act = (gate / (1.0 + jnp.exp(-gate)) * up).astype(BF16)
    acc_ref[...] += _dot(act, wo_ref[0].astype(BF16))

    if final:
        @pl.when(k == pl.num_programs(1) - 1)
        def _():
            acc_ref[...] = _rms(acc_ref[...], g_ref[...])


def _ffn(h2d, x2d, layer, w_in, w_up, w_out, g_final, final):
    n, d = x2d.shape
    f = w_out.shape[1]
    tm, tf = TM_FFN, TF_FFN
    nf = f // tf
    assert nf >= 2 and n % tm == 0 and f % tf == 0
    row = lambda i, k: (i, 0)
    return pl.pallas_call(
        functools.partial(_ffn_body, final, nf // 2),
        grid=(n // tm, nf),
        in_specs=[
            pl.BlockSpec((tm, d), row),
            pl.BlockSpec(memory_space=pl.ANY),
            pl.BlockSpec((1, d, tf), lambda i, k: (layer, 0, k)),
            pl.BlockSpec((1, d, tf), lambda i, k: (layer, 0, k)),
            pl.BlockSpec((1, tf, d), lambda i, k: (layer, k, 0)),
            _const_spec((1, d)),
        ],
        out_specs=pl.BlockSpec((tm, d), row),
        out_shape=jax.ShapeDtypeStruct((n, d), F32),
        scratch_shapes=[pltpu.VMEM((tm, d), F32), pltpu.SemaphoreType.DMA(())],
        compiler_params=_params(("arbitrary", "arbitrary"), VMEM_LIMIT_FFN),
        name="ffn_final" if final else "ffn",
    )(h2d, x2d, w_in, w_up, w_out, g_final)


def _mla_proj_body(x_ref, gm_ref, pos_ref, win_ref, gq_ref, gkv_ref, wqn_ref, wqr_ref, wk_ref, wv_ref,
                   freq_ref, sign_ref, q_ref, k_ref, v_ref):
    ts = x_ref.shape[0]
    rq = gq_ref.shape[1]
    rkv = gkv_ref.shape[1]
    heads = v_ref.shape[1] // V_HEAD_DIM
    half = QK_ROPE_DIM // 2
    h = _rms(x_ref[...], gm_ref[0]).astype(BF16)
    c = _dot(h, win_ref[...])
    cq = _rms(c[:, :rq], gq_ref[...]).astype(BF16)
    ckv = _rms(c[:, rq:rq + rkv], gkv_ref[...]).astype(BF16)

    ang = pos_ref[...] * freq_ref[...]
    cos = jnp.cos(ang)
    sin_signed = jnp.sin(ang) * sign_ref[...]
    lane = lax.broadcasted_iota(jnp.int32, (ts, LANES), 1)
    first_half = (lane & (QK_ROPE_DIM - 1)) < half
    low_lanes = lane < QK_ROPE_DIM

    def rope(x):
        swapped = jnp.where(first_half,
                            pltpu.roll(x, LANES - half, axis=1),
                            pltpu.roll(x, half, axis=1))
        return x * cos + swapped * sin_signed

    k_rope = rope(c[:, rq + rkv:]).astype(BF16)
    qr = _dot(cq, wqr_ref[...])
    for pair in range(heads // 2):
        r = rope(qr[:, pair * LANES:(pair + 1) * LANES])
        for j, part in enumerate((r, pltpu.roll(r, QK_ROPE_DIM, axis=1))):
            dst = (2 * pair + j) * QK_PAD_DIM + QK_NOPE_DIM
            q_ref[:, dst:dst + LANES] = jnp.where(low_lanes, part, 0.0).astype(BF16)
    group = 4
    for hc in range(heads // group):
        qn = _dot(cq, wqn_ref[:, hc * group * QK_NOPE_DIM:(hc + 1) * group * QK_NOPE_DIM])
        for j in range(group):
            dst = (hc * group + j) * QK_PAD_DIM
            q_ref[:, dst:dst + QK_NOPE_DIM] = qn[:, j * QK_NOPE_DIM:(j + 1) * QK_NOPE_DIM].astype(BF16)
    kn = _dot(ckv, wk_ref[...])
    for n in range(heads):
        dst = n * QK_PAD_DIM
        k_ref[:, dst:dst + QK_NOPE_DIM] = kn[:, n * QK_NOPE_DIM:(n + 1) * QK_NOPE_DIM].astype(BF16)
        k_ref[:, dst + QK_NOPE_DIM:dst + QK_PAD_DIM] = k_rope
    v_ref[...] = _dot(ckv, wv_ref[...]).astype(BF16)


def _mla_proj(x2d, gm, layer, pos, w_in, gq, gkv, wqn, wqr, wk, wv, freq, sign):
    n, d = x2d.shape
    ts = TS_PROJ
    heads = wv.shape[1] // V_HEAD_DIM
    qk_width = heads * QK_PAD_DIM
    row = lambda i: (i, 0)
    return pl.pallas_call(
        _mla_proj_body,
        grid=(n // ts,),
        in_specs=[
            pl.BlockSpec((ts, d), row),
            _layer_spec(gm, layer),
            pl.BlockSpec((ts, 1), row),
            _const_spec(w_in.shape),
            _const_spec(gq.shape),
            _const_spec(gkv.shape),
            _const_spec(wqn.shape),
            _const_spec(wqr.shape),
            _const_spec(wk.shape),
            _const_spec(wv.shape),
            _const_spec(freq.shape),
            _const_spec(sign.shape),
        ],
        out_specs=[
            pl.BlockSpec((ts, qk_width), row),
            pl.BlockSpec((ts, qk_width), row),
            pl.BlockSpec((ts, wv.shape[1]), row),
        ],
        out_shape=[
            jax.ShapeDtypeStruct((n, qk_width), BF16),
            jax.ShapeDtypeStruct((n, qk_width), BF16),
            jax.ShapeDtypeStruct((n, wv.shape[1]), BF16),
        ],
        compiler_params=_params(("arbitrary",)),
        name="mla_proj",
    )(x2d, gm, pos, w_in, gq, gkv, wqn, wqr, wk, wv, freq, sign)


def _attn_body(scale, q_ref, k_ref, v_ref, o_ref, vaug, s_buf, p_buf):
    seq = q_ref.shape[0]
    tq = TQ_ATTN
    nq = seq // tq
    dk = QK_PAD_DIM
    dv = V_HEAD_DIM
    heads = v_ref.shape[1] // dv
    c = scale * math.log2(math.e)
    row = lax.broadcasted_iota(jnp.int32, (tq, tq), 0)
    col = lax.broadcasted_iota(jnp.int32, (tq, tq), 1)
    causal = col <= row
    for h in range(heads):
        vaug[:, 2 * h * dv:(2 * h + 1) * dv] = v_ref[:, h * dv:(h + 1) * dv]
        vaug[:, (2 * h + 1) * dv:(2 * h + 2) * dv] = jnp.ones((seq, dv), BF16)

    tiles = []
    for h in range(heads):
        order = range(nq) if h % 2 == 0 else range(nq - 1, -1, -1)
        tiles += [(h, qi) for qi in order]

    scaled_row_max = {}

    def scores(t):
        h, qi = tiles[t]
        lo, nk, b = qi * tq, (qi + 1) * tq, t % 2
        s = _dot_nt(q_ref[lo:nk, h * dk:(h + 1) * dk], k_ref[0:nk, h * dk:(h + 1) * dk])
        s_diag = jnp.where(causal, s[:, lo:nk], MASK_VALUE)
        s_buf[b, :, lo:nk] = s_diag
        m_lanes = s_diag[:, 0:LANES]
        for j in range(LANES, tq, LANES):
            m_lanes = jnp.maximum(m_lanes, s_diag[:, j:j + LANES])
        if qi > 0:
            s_buf[b, :, 0:lo] = s[:, 0:lo]
            for j in range(0, lo, LANES):
                m_lanes = jnp.maximum(m_lanes, s[:, j:j + LANES])
        scaled_row_max[t] = jnp.max(m_lanes, axis=-1, keepdims=True) * c

    def softmax(t):
        _, qi = tiles[t]
        nk, b = (qi + 1) * tq, t % 2
        mc = scaled_row_max.pop(t)
        for j in range(0, nk, SOFTMAX_CHUNK):
            s = s_buf[b, :, j:j + SOFTMAX_CHUNK]
            p_buf[b, :, j:j + SOFTMAX_CHUNK] = jnp.exp2(s * c - mc).astype(BF16)

    def weighted_values(t):
        h, qi = tiles[t]
        lo, nk, b = qi * tq, (qi + 1) * tq, t % 2
        acc = _dot(p_buf[b, :, 0:nk], vaug[0:nk, 2 * h * dv:(2 * h + 2) * dv])
        o_ref[lo:nk, h * dv:(h + 1) * dv] = (acc[:, 0:dv] / acc[:, dv:2 * dv]).astype(BF16)

    scores(0)
    for t in range(len(tiles)):
        if t + 1 < len(tiles):
            scores(t + 1)
        softmax(t)
        weighted_values(t)


def _attention(q, k, v, batch, heads):
    n = q.shape[0]
    seq = n // batch
    hp = HEADS_PER_ATTN_STEP
    scale = 1.0 / math.sqrt(QK_NOPE_DIM + QK_ROPE_DIM)
    bh = lambda b, h: (b, h)
    return pl.pallas_call(
        functools.partial(_attn_body, scale),
        grid=(batch, heads // hp),
        in_specs=[
            pl.BlockSpec((seq, hp * QK_PAD_DIM), bh),
            pl.BlockSpec((seq, hp * QK_PAD_DIM), bh),
            pl.BlockSpec((seq, hp * V_HEAD_DIM), bh),
        ],
        out_specs=pl.BlockSpec((seq, hp * V_HEAD_DIM), bh),
        out_shape=jax.ShapeDtypeStruct((n, heads * V_HEAD_DIM), BF16),
        scratch_shapes=[
            pltpu.VMEM((seq, 2 * hp * V_HEAD_DIM), BF16),
            pltpu.VMEM((2, TQ_ATTN, seq), F32),
            pltpu.VMEM((2, TQ_ATTN, seq), BF16),
        ],
        compiler_params=_params(("arbitrary", "arbitrary")),
        name="mla_attention",
    )(q, k, v)


def _split_head_columns(w, head_dim, split):
    r = w.shape[0]
    heads = w.shape[1] // head_dim
    w = w.astype(BF16).reshape(r, heads, head_dim)
    a = w[:, :, :split].reshape(r, heads * split)
    b = w[:, :, split:].reshape(r, heads * (head_dim - split))
    return a, b


def _rope_rows():
    half = QK_ROPE_DIM // 2
    inv_freq = 1.0 / (ROPE_THETA ** (jnp.arange(0, QK_ROPE_DIM, 2, dtype=F32) / QK_ROPE_DIM))
    groups = LANES // QK_ROPE_DIM
    freq = jnp.tile(jnp.concatenate([inv_freq, inv_freq]), groups).reshape(1, LANES)
    sign = jnp.tile(jnp.concatenate([-jnp.ones((half,), F32), jnp.ones((half,), F32)]),
                    groups).reshape(1, LANES)
    return freq, sign


def kernel(x, mem, positions, norm_mix_g, norm_xattn_g, norm_mem_g, norm_ffn_g, pool_w, pool_scale, mla_w_in, mla_q_norm_g, mla_w_q_up, mla_kv_norm_g, mla_w_kv_up, mla_w_out, xattn_w_q, xattn_w_kv, xattn_w_o, ffn_w_in, ffn_w_out, final_norm_g):
    batch, seq, d = x.shape
    depth = norm_mix_g.shape[0]
    n_mixers = 2
    row = lambda g: g.reshape(1, -1)
    rows = lambda g: g.reshape(g.shape[0], 1, g.shape[1])

    gm, gx, gf = rows(norm_mix_g), rows(norm_xattn_g), rows(norm_ffn_g)
    wq_x, wo_x = xattn_w_q.astype(BF16), xattn_w_o.astype(BF16)
    w_ffn_up = ffn_w_in[:, :, ffn_w_out.shape[1]:].astype(BF16)
    w_pool, s_pool = pool_w.astype(BF16), rows(pool_scale)
    w_mla_out = mla_w_out.astype(BF16)

    k_mem, v_mem = _memkv(mem, norm_mem_g, xattn_w_kv.astype(BF16))
    freq, sign = _rope_rows()
    pos = positions.reshape(batch * seq, 1).astype(F32)

    x2d = x.reshape(batch * seq, d)
    for i in range(depth):
        j = i // n_mixers
        xattn_args = (gx, wq_x, k_mem, v_mem, wo_x, gf)
        if i % n_mixers == 0:
            x2d, h_ffn = _pool_xattn(x2d, batch, i, j, gm, w_pool, s_pool, *xattn_args)
        else:
            rq = mla_q_norm_g.shape[1]
            rkv = mla_kv_norm_g.shape[1]
            w_in = mla_w_in[j].astype(BF16)
            w_in = jnp.pad(w_in, ((0, 0), (0, rq + rkv + LANES - w_in.shape[1])))
            wqn, wqr = _split_head_columns(mla_w_q_up[j], QK_NOPE_DIM + QK_ROPE_DIM, QK_NOPE_DIM)
            wk, wv = _split_head_columns(mla_w_kv_up[j], QK_NOPE_DIM + V_HEAD_DIM, QK_NOPE_DIM)
            heads = wv.shape[1] // V_HEAD_DIM
            q, k, v = _mla_proj(x2d, gm, i, pos, w_in, row(mla_q_norm_g[j]), row(mla_kv_norm_g[j]),
                                wqn, wqr, wk, wv, freq, sign)
            o = _attention(q, k, v, batch, heads)
            x2d, h_ffn = _oproj_xattn(x2d, o, batch, i, j, w_mla_out, *xattn_args)
        x2d = _ffn(h_ffn, x2d, i, ffn_w_in, w_ffn_up, ffn_w_out, row(final_norm_g), i == depth - 1)
    return x2d.reshape(batch, seq, d)
```

```python
import functools
import math

import jax
import jax.numpy as jnp
from jax import lax
from jax.experimental import pallas as pl
from jax.experimental.pallas import tpu as pltpu

F32 = jnp.float32
BF16 = jnp.bfloat16

NORM_EPS = 1e-6
MASK_VALUE = -1e30
ROPE_THETA = 10000.0
POOL_WINDOWS = (2, 4, 8, 16)
QK_NOPE_DIM = 128
QK_ROPE_DIM = 64
V_HEAD_DIM = 128
XATTN_HEADS = 4

LANES = 128
QK_PAD_DIM = 256
POOL_HALO = 32

TS_MIX = 512
TS_PROJ = 512
TQ_ATTN = 256
HEADS_PER_ATTN_STEP = 2
SOFTMAX_CHUNK = 256
TM_FFN = 1024
TF_FFN = 512
FFN_SUBCHUNK = 256

VMEM_LIMIT = 48 * 1024 * 1024
VMEM_LIMIT_FFN = 63 * 1024 * 1024


def _rms(x, g):
    var = jnp.mean(x * x, axis=-1, keepdims=True)
    return x * lax.rsqrt(var + NORM_EPS) * g


def _dot(a, b):
    return jnp.dot(a, b, preferred_element_type=F32)


def _dot_nt(a, b):
    return lax.dot_general(a, b, (((1,), (1,)), ((), ())), preferred_element_type=F32)


def _const_spec(shape):
    nd = len(shape)
    return pl.BlockSpec(shape, lambda *_: (0,) * nd)


def _layer_spec(stacked, layer):
    tail = stacked.shape[1:]
    return pl.BlockSpec((1,) + tail, lambda *_: (layer,) + (0,) * len(tail))


def _params(sem, vmem_limit=VMEM_LIMIT):
    return pltpu.CompilerParams(dimension_semantics=sem, vmem_limit_bytes=vmem_limit)


def _memkv_body(mem_ref, g_ref, w_ref, k_ref, v_ref):
    xw = k_ref.shape[-1]
    hd = xw // XATTN_HEADS
    mn = _rms(mem_ref[0], g_ref[0]).astype(BF16)
    kv = _dot(mn, w_ref[0])
    k_ref[0, 0] = kv[:, :xw].astype(BF16)
    ones = jnp.ones((kv.shape[0], hd), BF16)
    for n in range(XATTN_HEADS):
        v_ref[0, 0, :, 2 * n * hd:(2 * n + 1) * hd] = kv[:, xw + n * hd:xw + (n + 1) * hd].astype(BF16)
        v_ref[0, 0, :, (2 * n + 1) * hd:(2 * n + 2) * hd] = ones


def _memkv(mem, g, w_kv):
    depth, d, xw2 = w_kv.shape
    b, m, _ = mem.shape
    xw = xw2 // 2
    return pl.pallas_call(
        _memkv_body,
        grid=(depth, b),
        in_specs=[
            pl.BlockSpec((1, m, d), lambda l, i: (i, 0, 0)),
            pl.BlockSpec((1, 1, d), lambda l, i: (l, 0, 0)),
            pl.BlockSpec((1, d, xw2), lambda l, i: (l, 0, 0)),
        ],
        out_specs=[
            pl.BlockSpec((1, 1, m, xw), lambda l, i: (l, i, 0, 0)),
            pl.BlockSpec((1, 1, m, xw2), lambda l, i: (l, i, 0, 0)),
        ],
        out_shape=[jax.ShapeDtypeStruct((depth, b, m, xw), BF16),
                   jax.ShapeDtypeStruct((depth, b, m, xw2), BF16)],
        compiler_params=_params(("arbitrary", "arbitrary")),
        name="memkv",
    )(mem, g.reshape(depth, 1, d), w_kv)


def _xattn_tail(x1, gx_ref, wq_ref, k_ref, v_ref, wo_ref, gf_ref, x2_ref, h3_ref):
    xw = wq_ref.shape[-1]
    hd = xw // XATTN_HEADS
    c = math.log2(math.e) / math.sqrt(hd)
    h2 = _rms(x1, gx_ref[0]).astype(BF16)
    q = _dot(h2, wq_ref[0]).astype(BF16)
    heads = []
    for n in range(XATTN_HEADS):
        sl = slice(n * hd, (n + 1) * hd)
        s = _dot_nt(q[:, sl], k_ref[0, 0, :, sl])
        mc = jnp.max(s, axis=-1, keepdims=True) * c
        p = jnp.exp2(s * c - mc).astype(BF16)
        acc = _dot(p, v_ref[0, 0, :, 2 * n * hd:(2 * n + 2) * hd])
        heads.append((acc[:, :hd] / acc[:, hd:]).astype(BF16))
    o = jnp.concatenate(heads, axis=-1)
    x2 = x1 + _dot(o, wo_ref[0])
    x2_ref[...] = x2
    h3_ref[...] = _rms(x2, gf_ref[0]).astype(BF16)


def _pool_xattn_body(x_ref, halo_ref, gm_ref, pw_ref, ps_ref,
                     gx_ref, wq_ref, k_ref, v_ref, wo_ref, gf_ref,
                     x2_ref, h3_ref, buf_a, buf_b):
    ts, d = x_ref.shape
    dg = d // len(POOL_WINDOWS)
    rows = ts + POOL_HALO
    i = pl.program_id(1)
    x = x_ref[...]
    hx = _rms(x, gm_ref[0])
    hh = _rms(halo_ref[...], gm_ref[0]) * (i > 0).astype(F32)
    t = i * ts + lax.broadcasted_iota(jnp.int32, (ts, 1), 0)
    x1 = []
    for g, w in enumerate(POOL_WINDOWS):
        sl = slice(g * dg, (g + 1) * dg)
        buf_a[0:POOL_HALO, :] = hh[:, sl]
        buf_a[POOL_HALO:rows, :] = hx[:, sl]
        src, dst, lo, shift = buf_a, buf_b, 0, w // 2
        while shift > 1:
            lo += 8
            dst[lo:rows, :] = src[lo:rows, :] + src[lo - shift:rows - shift, :]
            src, dst, shift = dst, src, shift // 2
        wsum = src[POOL_HALO:rows, :] + src[POOL_HALO - 1:rows - 1, :]
        inv_cnt = 1.0 / jnp.minimum(t + 1, w).astype(F32)
        pooled = (wsum * inv_cnt - hx[:, sl]).astype(BF16)
        x1.append(x[:, sl] + _dot(pooled, pw_ref[0, g]) * ps_ref[0, :, sl])
    _xattn_tail(jnp.concatenate(x1, axis=-1),
                gx_ref, wq_ref, k_ref, v_ref, wo_ref, gf_ref, x2_ref, h3_ref)


def _pool_xattn(x2d, batch, layer, pool_layer, gm, pool_w, pool_scale, gx, wq, k_mem, v_mem, wo, gf):
    n, d = x2d.shape
    seq = n // batch
    ts = TS_MIX
    nt = seq // ts
    hb = ts // POOL_HALO
    dg = pool_w.shape[-1]
    _, _, m, xw = k_mem.shape
    row = lambda b, i: (b * nt + i, 0)
    halo = lambda b, i: (jnp.maximum((b * nt + i) * hb - 1, 0), 0)
    memkv = lambda b, i: (layer, b, 0, 0)
    return pl.pallas_call(
        _pool_xattn_body,
        grid=(batch, nt),
        in_specs=[
            pl.BlockSpec((ts, d), row),
            pl.BlockSpec((POOL_HALO, d), halo),
            _layer_spec(gm, layer),
            _layer_spec(pool_w, pool_layer),
            _layer_spec(pool_scale, pool_layer),
            _layer_spec(gx, layer),
            _layer_spec(wq, layer),
            pl.BlockSpec((1, 1, m, xw), memkv),
            pl.BlockSpec((1, 1, m, 2 * xw), memkv),
            _layer_spec(wo, layer),
            _layer_spec(gf, layer),
        ],
        out_specs=[pl.BlockSpec((ts, d), row), pl.BlockSpec((ts, d), row)],
        out_shape=[jax.ShapeDtypeStruct((n, d), F32), jax.ShapeDtypeStruct((n, d), BF16)],
        scratch_shapes=[
            pltpu.VMEM((ts + POOL_HALO, dg), F32),
            pltpu.VMEM((ts + POOL_HALO, dg), F32),
        ],
        compiler_params=_params(("arbitrary", "arbitrary")),
        name="pool_xattn",
    )(x2d, x2d, gm, pool_w, pool_scale, gx, wq, k_mem, v_mem, wo, gf)


def _oproj_xattn_body(x_ref, o_ref, wout_ref,
                      gx_ref, wq_ref, k_ref, v_ref, wo_ref, gf_ref, x2_ref, h3_ref):
    x1 = x_ref[...] + _dot(o_ref[...], wout_ref[0])
    _xattn_tail(x1, gx_ref, wq_ref, k_ref, v_ref, wo_ref, gf_ref, x2_ref, h3_ref)


def _oproj_xattn(x2d, o2d, batch, layer, mla_layer, w_out, gx, wq, k_mem, v_mem, wo, gf):
    n, d = x2d.shape
    seq = n // batch
    ts = TS_MIX
    nt = seq // ts
    _, _, m, xw = k_mem.shape
    row = lambda b, i: (b * nt + i, 0)
    memkv = lambda b, i: (layer, b, 0, 0)
    return pl.pallas_call(
        _oproj_xattn_body,
        grid=(batch, nt),
        in_specs=[
            pl.BlockSpec((ts, d), row),
            pl.BlockSpec((ts, o2d.shape[1]), row),
            _layer_spec(w_out, mla_layer),
            _layer_spec(gx, layer),
            _layer_spec(wq, layer),
            pl.BlockSpec((1, 1, m, xw), memkv),
            pl.BlockSpec((1, 1, m, 2 * xw), memkv),
            _layer_spec(wo, layer),
            _layer_spec(gf, layer),
        ],
        out_specs=[pl.BlockSpec((ts, d), row), pl.BlockSpec((ts, d), row)],
        out_shape=[jax.ShapeDtypeStruct((n, d), F32), jax.ShapeDtypeStruct((n, d), BF16)],
        compiler_params=_params(("arbitrary", "arbitrary")),
        name="oproj_xattn",
    )(x2d, o2d, w_out, gx, wq, k_mem, v_mem, wo, gf)


def _ffn_body(final, prefetch_step, h_ref, x_hbm, wg_ref, wu_ref, wo_ref, g_ref, acc_ref, xbuf, sem):
    i = pl.program_id(0)
    k = pl.program_id(1)
    tm = acc_ref.shape[0]

    def x_copy(tile):
        start = tile * tm
        if not isinstance(start, int):
            start = pl.multiple_of(start, tm)
        return pltpu.make_async_copy(x_hbm.at[pl.ds(start, tm), :], xbuf, sem)

    @pl.when(jnp.logical_and(i == 0, k == 0))
    def _():
        x_copy(0).start()

    @pl.when(k == 0)
    def _():
        x_copy(i).wait()
        acc_ref[...] = xbuf[...]

    @pl.when(jnp.logical_and(k == prefetch_step, i + 1 < pl.num_programs(0)))
    def _():
        x_copy(i + 1).start()

    h = h_ref[...]
    for c0 in range(0, wg_ref.shape[-1], FFN_SUBCHUNK):
        cols = slice(c0, c0 + FFN_SUBCHUNK)
        gate = _dot(h, wg_ref[0, :, cols].astype(BF16))
        up = _dot(h, wu_ref[0, :, cols].astype(BF16))
        act = (gate / (1.0 + jnp.exp(-gate)) * up).astype(BF16)
        acc_ref[...] += _dot(act, wo_ref[0, cols, :].astype(BF16))

    if final:
        @pl.when(k == pl.num_programs(1) - 1)
        def _():
            acc_ref[...] = _rms(acc_ref[...], g_ref[...])


def _ffn(h2d, x2d, layer, w_in, w_out, g_final, final):
    n, d = x2d.shape
    f = w_out.shape[1]
    tm, tf = TM_FFN, TF_FFN
    nf = f // tf
    assert nf >= 2 and n % tm == 0 and f % tf == 0
    row = lambda i, k: (i, 0)
    return pl.pallas_call(
        functools.partial(_ffn_body, final, nf // 2),
        grid=(n // tm, nf),
        in_specs=[
            pl.BlockSpec((tm, d), row),
            pl.BlockSpec(memory_space=pl.ANY),
            pl.BlockSpec((1, d, tf), lambda i, k: (layer, 0, k)),
            pl.BlockSpec((1, d, tf), lambda i, k: (layer, 0, nf + k)),
            pl.BlockSpec((1, tf, d), lambda i, k: (layer, k, 0)),
            _const_spec((1, d)),
        ],
        out_specs=pl.BlockSpec((tm, d), row),
        out_shape=jax.ShapeDtypeStruct((n, d), F32),
        scratch_shapes=[pltpu.VMEM((tm, d), F32), pltpu.SemaphoreType.DMA(())],
        compiler_params=_params(("arbitrary", "arbitrary"), VMEM_LIMIT_FFN),
        name="ffn_final" if final else "ffn",
    )(h2d, x2d, w_in, w_in, w_out, g_final)


def _mla_proj_body(x_ref, gm_ref, pos_ref, win_ref, gq_ref, gkv_ref, wqn_ref, wqr_ref, wk_ref, wv_ref,
                   freq_ref, sign_ref, q_ref, k_ref, v_ref):
    ts = x_ref.shape[0]
    rq = gq_ref.shape[1]
    rkv = gkv_ref.shape[1]
    heads = v_ref.shape[1] // V_HEAD_DIM
    half = QK_ROPE_DIM // 2
    h = _rms(x_ref[...], gm_ref[0]).astype(BF16)
    c = _dot(h, win_ref[...])
    cq = _rms(c[:, :rq], gq_ref[...]).astype(BF16)
    ckv = _rms(c[:, rq:rq + rkv], gkv_ref[...]).astype(BF16)

    ang = pos_ref[...] * freq_ref[...]
    cos = jnp.cos(ang)
    sin_signed = jnp.sin(ang) * sign_ref[...]
    lane = lax.broadcasted_iota(jnp.int32, (ts, LANES), 1)
    first_half = (lane & (QK_ROPE_DIM - 1)) < half
    low_lanes = lane < QK_ROPE_DIM

    def rope(x):
        swapped = jnp.where(first_half,
                            pltpu.roll(x, LANES - half, axis=1),
                            pltpu.roll(x, half, axis=1))
        return x * cos + swapped * sin_signed

    k_rope = rope(c[:, rq + rkv:]).astype(BF16)
    qr = _dot(cq, wqr_ref[...])
    for pair in range(heads // 2):
        r = rope(qr[:, pair * LANES:(pair + 1) * LANES])
        for j, part in enumerate((r, pltpu.roll(r, QK_ROPE_DIM, axis=1))):
            dst = (2 * pair + j) * QK_PAD_DIM + QK_NOPE_DIM
            q_ref[:, dst:dst + LANES] = jnp.where(low_lanes, part, 0.0).astype(BF16)
    group = 4
    for hc in range(heads // group):
        qn = _dot(cq, wqn_ref[:, hc * group * QK_NOPE_DIM:(hc + 1) * group * QK_NOPE_DIM])
        for j in range(group):
            dst = (hc * group + j) * QK_PAD_DIM
            q_ref[:, dst:dst + QK_NOPE_DIM] = qn[:, j * QK_NOPE_DIM:(j + 1) * QK_NOPE_DIM].astype(BF16)
    kn = _dot(ckv, wk_ref[...])
    for n in range(heads):
        dst = n * QK_PAD_DIM
        k_ref[:, dst:dst + QK_NOPE_DIM] = kn[:, n * QK_NOPE_DIM:(n + 1) * QK_NOPE_DIM].astype(BF16)
        k_ref[:, dst + QK_NOPE_DIM:dst + QK_PAD_DIM] = k_rope
    v_ref[...] = _dot(ckv, wv_ref[...]).astype(BF16)


def _mla_proj(x2d, gm, layer, pos, w_in, gq, gkv, wqn, wqr, wk, wv, freq, sign):
    n, d = x2d.shape
    ts = TS_PROJ
    heads = wv.shape[1] // V_HEAD_DIM
    qk_width = heads * QK_PAD_DIM
    row = lambda i: (i, 0)
    return pl.pallas_call(
        _mla_proj_body,
        grid=(n // ts,),
        in_specs=[
            pl.BlockSpec((ts, d), row),
            _layer_spec(gm, layer),
            pl.BlockSpec((ts, 1), row),
            _const_spec(w_in.shape),
            _const_spec(gq.shape),
            _const_spec(gkv.shape),
            _const_spec(wqn.shape),
            _const_spec(wqr.shape),
            _const_spec(wk.shape),
            _const_spec(wv.shape),
            _const_spec(freq.shape),
            _const_spec(sign.shape),
        ],
        out_specs=[
            pl.BlockSpec((ts, qk_width), row),
            pl.BlockSpec((ts, qk_width), row),
            pl.BlockSpec((ts, wv.shape[1]), row),
        ],
        out_shape=[
            jax.ShapeDtypeStruct((n, qk_width), BF16),
            jax.ShapeDtypeStruct((n, qk_width), BF16),
            jax.ShapeDtypeStruct((n, wv.shape[1]), BF16),
        ],
        compiler_params=_params(("arbitrary",)),
        name="mla_proj",
    )(x2d, gm, pos, w_in, gq, gkv, wqn, wqr, wk, wv, freq, sign)


def _attn_body(scale, q_ref, k_ref, v_ref, o_ref, vaug, s_buf, p_buf):
    seq = q_ref.shape[0]
    tq = TQ_ATTN
    nq = seq // tq
    dk = QK_PAD_DIM
    dv = V_HEAD_DIM
    heads = v_ref.shape[1] // dv
    c = scale * math.log2(math.e)
    row = lax.broadcasted_iota(jnp.int32, (tq, tq), 0)
    col = lax.broadcasted_iota(jnp.int32, (tq, tq), 1)
    causal = col <= row
    for h in range(heads):
        vaug[:, 2 * h * dv:(2 * h + 1) * dv] = v_ref[:, h * dv:(h + 1) * dv]
        vaug[:, (2 * h + 1) * dv:(2 * h + 2) * dv] = jnp.ones((seq, dv), BF16)

    tiles = []
    for h in range(heads):
        order = range(nq) if h % 2 == 0 else range(nq - 1, -1, -1)
        tiles += [(h, qi) for qi in order]

    scaled_row_max = {}

    def scores(t):
        h, qi = tiles[t]
        lo, nk, b = qi * tq, (qi + 1) * tq, t % 2
        s = _dot_nt(q_ref[lo:nk, h * dk:(h + 1) * dk], k_ref[0:nk, h * dk:(h + 1) * dk])
        s_diag = jnp.where(causal, s[:, lo:nk], MASK_VALUE)
        s_buf[b, :, lo:nk] = s_diag
        m_lanes = s_diag[:, 0:LANES]
        for j in range(LANES, tq, LANES):
            m_lanes = jnp.maximum(m_lanes, s_diag[:, j:j + LANES])
        if qi > 0:
            s_buf[b, :, 0:lo] = s[:, 0:lo]
            for j in range(0, lo, LANES):
                m_lanes = jnp.maximum(m_lanes, s[:, j:j + LANES])
        scaled_row_max[t] = jnp.max(m_lanes, axis=-1, keepdims=True) * c

    def softmax(t):
        _, qi = tiles[t]
        nk, b = (qi + 1) * tq, t % 2
        mc = scaled_row_max.pop(t)
        for j in range(0, nk, SOFTMAX_CHUNK):
            s = s_buf[b, :, j:j + SOFTMAX_CHUNK]
            p_buf[b, :, j:j + SOFTMAX_CHUNK] = jnp.exp2(s * c - mc).astype(BF16)

    def weighted_values(t):
        h, qi = tiles[t]
        lo, nk, b = qi * tq, (qi + 1) * tq, t % 2
        acc = _dot(p_buf[b, :, 0:nk], vaug[0:nk, 2 * h * dv:(2 * h + 2) * dv])
        o_ref[lo:nk, h * dv:(h + 1) * dv] = (acc[:, 0:dv] / acc[:, dv:2 * dv]).astype(BF16)

    scores(0)
    for t in range(len(tiles)):
        if t + 1 < len(tiles):
            scores(t + 1)
        softmax(t)
        weighted_values(t)


def _attention(q, k, v, batch, heads):
    n = q.shape[0]
    seq = n // batch
    hp = HEADS_PER_ATTN_STEP
    scale = 1.0 / math.sqrt(QK_NOPE_DIM + QK_ROPE_DIM)
    bh = lambda b, h: (b, h)
    return pl.pallas_call(
        functools.partial(_attn_body, scale),
        grid=(batch, heads // hp),
        in_specs=[
            pl.BlockSpec((seq, hp * QK_PAD_DIM), bh),
            pl.BlockSpec((seq, hp * QK_PAD_DIM), bh),
            pl.BlockSpec((seq, hp * V_HEAD_DIM), bh),
        ],
        out_specs=pl.BlockSpec((seq, hp * V_HEAD_DIM), bh),
        out_shape=jax.ShapeDtypeStruct((n, heads * V_HEAD_DIM), BF16),
        scratch_shapes=[
            pltpu.VMEM((seq, 2 * hp * V_HEAD_DIM), BF16),
            pltpu.VMEM((2, TQ_ATTN, seq), F32),
            pltpu.VMEM((2, TQ_ATTN, seq), BF16),
        ],
        compiler_params=_params(("arbitrary", "arbitrary")),
        name="mla_attention",
    )(q, k, v)


def _split_head_columns(w, head_dim, split):
    r = w.shape[0]
    heads = w.shape[1] // head_dim
    w = w.astype(BF16).reshape(r, heads, head_dim)
    a = w[:, :, :split].reshape(r, heads * split)
    b = w[:, :, split:].reshape(r, heads * (head_dim - split))
    return a, b


def _rope_rows():
    half = QK_ROPE_DIM // 2
    inv_freq = 1.0 / (ROPE_THETA ** (jnp.arange(0, QK_ROPE_DIM, 2, dtype=F32) / QK_ROPE_DIM))
    groups = LANES // QK_ROPE_DIM
    freq = jnp.tile(jnp.concatenate([inv_freq, inv_freq]), groups).reshape(1, LANES)
    sign = jnp.tile(jnp.concatenate([-jnp.ones((half,), F32), jnp.ones((half,), F32)]),
                    groups).reshape(1, LANES)
    return freq, sign


def kernel(x, mem, positions, norm_mix_g, norm_xattn_g, norm_mem_g, norm_ffn_g, pool_w, pool_scale, mla_w_in, mla_q_norm_g, mla_w_q_up, mla_kv_norm_g, mla_w_kv_up, mla_w_out, xattn_w_q, xattn_w_kv, xattn_w_o, ffn_w_in, ffn_w_out, final_norm_g):
    batch, seq, d = x.shape
    depth = norm_mix_g.shape[0]
    n_mixers = 2
    row = lambda g: g.reshape(1, -1)
    rows = lambda g: g.reshape(g.shape[0], 1, g.shape[1])

    gm, gx, gf = rows(norm_mix_g), rows(norm_xattn_g), rows(norm_ffn_g)
    wq_x, wo_x = xattn_w_q.astype(BF16), xattn_w_o.astype(BF16)
    w_pool, s_pool = pool_w.astype(BF16), rows(pool_scale)
    w_mla_out = mla_w_out.astype(BF16)

    k_mem, v_mem = _memkv(mem, norm_mem_g, xattn_w_kv.astype(BF16))
    freq, sign = _rope_rows()
    pos = positions.reshape(batch * seq, 1).astype(F32)

    x2d = x.reshape(batch * seq, d)
    for i in range(depth):
        j = i // n_mixers
        xattn_args = (gx, wq_x, k_mem, v_mem, wo_x, gf)
        if i % n_mixers == 0:
            x2d, h_ffn = _pool_xattn(x2d, batch, i, j, gm, w_pool, s_pool, *xattn_args)
        else:
            rq = mla_q_norm_g.shape[1]
            rkv = mla_kv_norm_g.shape[1]
            w_in = mla_w_in[j].astype(BF16)
            w_in = jnp.pad(w_in, ((0, 0), (0, rq + rkv + LANES - w_in.shape[1])))
            wqn, wqr = _split_head_columns(mla_w_q_up[j], QK_NOPE_DIM + QK_ROPE_DIM, QK_NOPE_DIM)
            wk, wv = _split_head_columns(mla_w_kv_up[j], QK_NOPE_DIM + V_HEAD_DIM, QK_NOPE_DIM)
            heads = wv.shape[1] // V_HEAD_DIM
            q, k, v = _mla_proj(x2d, gm, i, pos, w_in, row(mla_q_norm_g[j]), row(mla_kv_norm_g[j]),
                                wqn, wqr, wk, wv, freq, sign)
            o = _attention(q, k, v, batch, heads)
            x2d, h_ffn = _oproj_xattn(x2d, o, batch, i, j, w_mla_out, *xattn_args)
        x2d = _ffn(h_ffn, x2d, i, ffn_w_in, ffn_w_out, row(final_norm_g), i == depth - 1)
    return x2d.reshape(batch, seq, d)
```

```python
import functools
import math

import jax
import jax.numpy as jnp
from jax import lax
from jax.experimental import pallas as pl
from jax.experimental.pallas import tpu as pltpu

F32 = jnp.float32
BF16 = jnp.bfloat16

NORM_EPS = 1e-6
MASK_VALUE = -1e30
ROPE_THETA = 10000.0
POOL_WINDOWS = (2, 4, 8, 16)
QK_NOPE_DIM = 128
QK_ROPE_DIM = 64
V_HEAD_DIM = 128
XATTN_HEADS = 4

LANES = 128
QK_PAD_DIM = 256
POOL_HALO = 32

TS_MIX = 512
TS_PROJ = 512
TQ_ATTN = 256
HEADS_PER_ATTN_STEP = 2
SOFTMAX_CHUNK = 256
TM_FFN = 1024
TF_FFN = 512
FFN_SUBCHUNK = 256

VMEM_LIMIT = 48 * 1024 * 1024
VMEM_LIMIT_OPROJ = 56 * 1024 * 1024
VMEM_LIMIT_FFN = 63 * 1024 * 1024


def _rms(x, g):
    var = jnp.mean(x * x, axis=-1, keepdims=True)
    return x * lax.rsqrt(var + NORM_EPS) * g


def _dot(a, b):
    return jnp.dot(a, b, preferred_element_type=F32)


def _dot_nt(a, b):
    return lax.dot_general(a, b, (((1,), (1,)), ((), ())), preferred_element_type=F32)


def _const_spec(shape):
    nd = len(shape)
    return pl.BlockSpec(shape, lambda *_: (0,) * nd, pipeline_mode=pl.Buffered(1))


def _layer_spec(stacked, layer):
    tail = stacked.shape[1:]
    return pl.BlockSpec((1,) + tail, lambda *_: (layer,) + (0,) * len(tail),
                        pipeline_mode=pl.Buffered(1))


def _params(sem, vmem_limit=VMEM_LIMIT):
    return pltpu.CompilerParams(dimension_semantics=sem, vmem_limit_bytes=vmem_limit)


def _memkv_body(mem_ref, g_ref, w_ref, k_ref, v_ref):
    xw = k_ref.shape[-1]
    hd = xw // XATTN_HEADS
    mn = _rms(mem_ref[0], g_ref[0]).astype(BF16)
    kv = _dot(mn, w_ref[0].astype(BF16))
    k_ref[0, 0] = kv[:, :xw].astype(BF16)
    ones = jnp.ones((kv.shape[0], hd), BF16)
    for n in range(XATTN_HEADS):
        v_ref[0, 0, :, 2 * n * hd:(2 * n + 1) * hd] = kv[:, xw + n * hd:xw + (n + 1) * hd].astype(BF16)
        v_ref[0, 0, :, (2 * n + 1) * hd:(2 * n + 2) * hd] = ones


def _memkv(mem, g, w_kv):
    depth, d, xw2 = w_kv.shape
    b, m, _ = mem.shape
    xw = xw2 // 2
    return pl.pallas_call(
        _memkv_body,
        grid=(depth, b),
        in_specs=[
            pl.BlockSpec((1, m, d), lambda l, i: (i, 0, 0)),
            pl.BlockSpec((1, 1, d), lambda l, i: (l, 0, 0)),
            pl.BlockSpec((1, d, xw2), lambda l, i: (l, 0, 0)),
        ],
        out_specs=[
            pl.BlockSpec((1, 1, m, xw), lambda l, i: (l, i, 0, 0)),
            pl.BlockSpec((1, 1, m, xw2), lambda l, i: (l, i, 0, 0)),
        ],
        out_shape=[jax.ShapeDtypeStruct((depth, b, m, xw), BF16),
                   jax.ShapeDtypeStruct((depth, b, m, xw2), BF16)],
        compiler_params=_params(("arbitrary", "arbitrary")),
        name="memkv",
    )(mem, g.reshape(depth, 1, d), w_kv)


def _xattn_tail(x1, gx_ref, wq_ref, k_ref, v_ref, wo_ref, gf_ref, x2_ref, h3_ref):
    xw = wq_ref.shape[-1]
    hd = xw // XATTN_HEADS
    c = math.log2(math.e) / math.sqrt(hd)
    h2 = _rms(x1, gx_ref[0]).astype(BF16)
    q = _dot(h2, wq_ref[0]).astype(BF16)
    heads = []
    for n in range(XATTN_HEADS):
        sl = slice(n * hd, (n + 1) * hd)
        s = _dot_nt(q[:, sl], k_ref[0, 0, :, sl])
        mc = jnp.max(s, axis=-1, keepdims=True) * c
        p = jnp.exp2(s * c - mc).astype(BF16)
        acc = _dot(p, v_ref[0, 0, :, 2 * n * hd:(2 * n + 2) * hd])
        heads.append((acc[:, :hd] / acc[:, hd:]).astype(BF16))
    o = jnp.concatenate(heads, axis=-1)
    x2 = x1 + _dot(o, wo_ref[0])
    x2_ref[...] = x2
    h3_ref[...] = _rms(x2, gf_ref[0]).astype(BF16)


def _pool_xattn_body(x_ref, halo_ref, gm_ref, pw_ref, ps_ref,
                     gx_ref, wq_ref, k_ref, v_ref, wo_ref, gf_ref,
                     x2_ref, h3_ref, buf_a, buf_b):
    ts, d = x_ref.shape
    dg = d // len(POOL_WINDOWS)
    rows = ts + POOL_HALO
    i = pl.program_id(1)
    x = x_ref[...]
    hx = _rms(x, gm_ref[0])
    hh = _rms(halo_ref[...], gm_ref[0]) * (i > 0).astype(F32)
    t = i * ts + lax.broadcasted_iota(jnp.int32, (ts, 1), 0)
    x1 = []
    for g, w in enumerate(POOL_WINDOWS):
        sl = slice(g * dg, (g + 1) * dg)
        buf_a[0:POOL_HALO, :] = hh[:, sl]
        buf_a[POOL_HALO:rows, :] = hx[:, sl]
        src, dst, lo, shift = buf_a, buf_b, 0, w // 2
        while shift > 1:
            lo += 8
            dst[lo:rows, :] = src[lo:rows, :] + src[lo - shift:rows - shift, :]
            src, dst, shift = dst, src, shift // 2
        wsum = src[POOL_HALO:rows, :] + src[POOL_HALO - 1:rows - 1, :]
        inv_cnt = 1.0 / jnp.minimum(t + 1, w).astype(F32)
        pooled = (wsum * inv_cnt - hx[:, sl]).astype(BF16)
        x1.append(x[:, sl] + _dot(pooled, pw_ref[0, g]) * ps_ref[0, :, sl])
    _xattn_tail(jnp.concatenate(x1, axis=-1),
                gx_ref, wq_ref, k_ref, v_ref, wo_ref, gf_ref, x2_ref, h3_ref)


def _pool_xattn(x2d, batch, layer, pool_layer, gm, pool_w, pool_scale, gx, wq, k_mem, v_mem, wo, gf):
    n, d = x2d.shape
    seq = n // batch
    ts = TS_MIX
    nt = seq // ts
    hb = ts // POOL_HALO
    dg = pool_w.shape[-1]
    _, _, m, xw = k_mem.shape
    row = lambda b, i: (b * nt + i, 0)
    halo = lambda b, i: (jnp.maximum((b * nt + i) * hb - 1, 0), 0)
    memkv = lambda b, i: (layer, b, 0, 0)
    return pl.pallas_call(
        _pool_xattn_body,
        grid=(batch, nt),
        in_specs=[
            pl.BlockSpec((ts, d), row),
            pl.BlockSpec((POOL_HALO, d), halo),
            _layer_spec(gm, layer),
            _layer_spec(pool_w, pool_layer),
            _layer_spec(pool_scale, pool_layer),
            _layer_spec(gx, layer),
            _layer_spec(wq, layer),
            pl.BlockSpec((1, 1, m, xw), memkv),
            pl.BlockSpec((1, 1, m, 2 * xw), memkv),
            _layer_spec(wo, layer),
            _layer_spec(gf, layer),
        ],
        out_specs=[pl.BlockSpec((ts, d), row), pl.BlockSpec((ts, d), row)],
        out_shape=[jax.ShapeDtypeStruct((n, d), F32), jax.ShapeDtypeStruct((n, d), BF16)],
        scratch_shapes=[
            pltpu.VMEM((ts + POOL_HALO, dg), F32),
            pltpu.VMEM((ts + POOL_HALO, dg), F32),
        ],
        compiler_params=_params(("arbitrary", "arbitrary")),
        name="pool_xattn",
    )(x2d, x2d, gm, pool_w, pool_scale, gx, wq, k_mem, v_mem, wo, gf)


def _oproj_xattn_body(x_ref, o_ref, wout_ref,
                      gx_ref, wq_ref, k_ref, v_ref, wo_ref, gf_ref, x2_ref, h3_ref):
    x1 = x_ref[...] + _dot(o_ref[...], wout_ref[0].astype(BF16))
    _xattn_tail(x1, gx_ref, wq_ref, k_ref, v_ref, wo_ref, gf_ref, x2_ref, h3_ref)


def _oproj_xattn(x2d, o2d, batch, layer, mla_layer, w_out, gx, wq, k_mem, v_mem, wo, gf):
    n, d = x2d.shape
    seq = n // batch
    ts = TS_MIX
    nt = seq // ts
    _, _, m, xw = k_mem.shape
    row = lambda b, i: (b * nt + i, 0)
    memkv = lambda b, i: (layer, b, 0, 0)
    return pl.pallas_call(
        _oproj_xattn_body,
        grid=(batch, nt),
        in_specs=[
            pl.BlockSpec((ts, d), row),
            pl.BlockSpec((ts, o2d.shape[1]), row),
            _layer_spec(w_out, mla_layer),
            _layer_spec(gx, layer),
            _layer_spec(wq, layer),
            pl.BlockSpec((1, 1, m, xw), memkv),
            pl.BlockSpec((1, 1, m, 2 * xw), memkv),
            _layer_spec(wo, layer),
            _layer_spec(gf, layer),
        ],
        out_specs=[pl.BlockSpec((ts, d), row), pl.BlockSpec((ts, d), row)],
        out_shape=[jax.ShapeDtypeStruct((n, d), F32), jax.ShapeDtypeStruct((n, d), BF16)],
        compiler_params=_params(("arbitrary", "arbitrary"), VMEM_LIMIT_OPROJ),
        name="oproj_xattn",
    )(x2d, o2d, w_out, gx, wq, k_mem, v_mem, wo, gf)


def _ffn_body(final, prefetch_step, h_ref, x_hbm, wg_ref, wu_ref, wo_ref, g_ref, acc_ref, xbuf, sem):
    i = pl.program_id(0)
    k = pl.program_id(1)
    tm = acc_ref.shape[0]

    def x_copy(tile):
        start = tile * tm
        if not isinstance(start, int):
            start = pl.multiple_of(start, tm)
        return pltpu.make_async_copy(x_hbm.at[pl.ds(start, tm), :], xbuf, sem)

    @pl.when(jnp.logical_and(i == 0, k == 0))
    def _():
        x_copy(0).start()

    @pl.when(k == 0)
    def _():
        x_copy(i).wait()
        acc_ref[...] = xbuf[...]

    @pl.when(jnp.logical_and(k == prefetch_step, i + 1 < pl.num_programs(0)))
    def _():
        x_copy(i + 1).start()

    h = h_ref[...]
    for c0 in range(0, wg_ref.shape[-1], FFN_SUBCHUNK):
        cols = slice(c0, c0 + FFN_SUBCHUNK)
        gate = _dot(h, wg_ref[0, :, cols].astype(BF16))
        up = _dot(h, wu_ref[0, :, cols].astype(BF16))
        act = (gate / (1.0 + jnp.exp(-gate)) * up).astype(BF16)
        acc_ref[...] += _dot(act, wo_ref[0, cols, :].astype(BF16))

    if final:
        @pl.when(k == pl.num_programs(1) - 1)
        def _():
            acc_ref[...] = _rms(acc_ref[...], g_ref[...])


def _ffn(h2d, x2d, layer, w_in, w_out, g_final, final):
    n, d = x2d.shape
    f = w_out.shape[1]
    tm, tf = TM_FFN, TF_FFN
    nf = f // tf
    assert nf >= 2 and n % tm == 0 and f % tf == 0
    row = lambda i, k: (i, 0)
    return pl.pallas_call(
        functools.partial(_ffn_body, final, nf // 2),
        grid=(n // tm, nf),
        in_specs=[
            pl.BlockSpec((tm, d), row),
            pl.BlockSpec(memory_space=pl.ANY),
            pl.BlockSpec((1, d, tf), lambda i, k: (layer, 0, k)),
            pl.BlockSpec((1, d, tf), lambda i, k: (layer, 0, nf + k)),
            pl.BlockSpec((1, tf, d), lambda i, k: (layer, k, 0)),
            _const_spec((1, d)),
        ],
        out_specs=pl.BlockSpec((tm, d), row),
        out_shape=jax.ShapeDtypeStruct((n, d), F32),
        scratch_shapes=[pltpu.VMEM((tm, d), F32), pltpu.SemaphoreType.DMA(())],
        compiler_params=_params(("arbitrary", "arbitrary"), VMEM_LIMIT_FFN),
        name="ffn_final" if final else "ffn",
    )(h2d, x2d, w_in, w_in, w_out, g_final)


def _mla_proj_body(x_ref, gm_ref, pos_ref, win_ref, gq_ref, gkv_ref, wqn_ref, wqr_ref, wk_ref, wv_ref,
                   freq_ref, sign_ref, q_ref, k_ref, v_ref):
    ts = x_ref.shape[0]
    rq = gq_ref.shape[1]
    rkv = gkv_ref.shape[1]
    heads = v_ref.shape[1] // V_HEAD_DIM
    half = QK_ROPE_DIM // 2
    h = _rms(x_ref[...], gm_ref[0]).astype(BF16)
    c = _dot(h, win_ref[...])
    cq = _rms(c[:, :rq], gq_ref[...]).astype(BF16)
    ckv = _rms(c[:, rq:rq + rkv], gkv_ref[...]).astype(BF16)

    ang = pos_ref[...] * freq_ref[...]
    cos = jnp.cos(ang)
    sin_signed = jnp.sin(ang) * sign_ref[...]
    lane = lax.broadcasted_iota(jnp.int32, (ts, LANES), 1)
    first_half = (lane & (QK_ROPE_DIM - 1)) < half
    low_lanes = lane < QK_ROPE_DIM

    def rope(x):
        swapped = jnp.where(first_half,
                            pltpu.roll(x, LANES - half, axis=1),
                            pltpu.roll(x, half, axis=1))
        return x * cos + swapped * sin_signed

    k_rope = rope(c[:, rq + rkv:]).astype(BF16)
    qr = _dot(cq, wqr_ref[...])
    for pair in range(heads // 2):
        r = rope(qr[:, pair * LANES:(pair + 1) * LANES])
        for j, part in enumerate((r, pltpu.roll(r, QK_ROPE_DIM, axis=1))):
            dst = (2 * pair + j) * QK_PAD_DIM + QK_NOPE_DIM
            q_ref[:, dst:dst + LANES] = jnp.where(low_lanes, part, 0.0).astype(BF16)
    group = 4
    for hc in range(heads // group):
        qn = _dot(cq, wqn_ref[:, hc * group * QK_NOPE_DIM:(hc + 1) * group * QK_NOPE_DIM])
        for j in range(group):
            dst = (hc * group + j) * QK_PAD_DIM
            q_ref[:, dst:dst + QK_NOPE_DIM] = qn[:, j * QK_NOPE_DIM:(j + 1) * QK_NOPE_DIM].astype(BF16)
    kn = _dot(ckv, wk_ref[...])
    for n in range(heads):
        dst = n * QK_PAD_DIM
        k_ref[:, dst:dst + QK_NOPE_DIM] = kn[:, n * QK_NOPE_DIM:(n + 1) * QK_NOPE_DIM].astype(BF16)
        k_ref[:, dst + QK_NOPE_DIM:dst + QK_PAD_DIM] = k_rope
    v_ref[...] = _dot(ckv, wv_ref[...]).astype(BF16)


def _mla_proj(x2d, gm, layer, pos, w_in, gq, gkv, wqn, wqr, wk, wv, freq, sign):
    n, d = x2d.shape
    ts = TS_PROJ
    heads = wv.shape[1] // V_HEAD_DIM
    qk_width = heads * QK_PAD_DIM
    row = lambda i: (i, 0)
    return pl.pallas_call(
        _mla_proj_body,
        grid=(n // ts,),
        in_specs=[
            pl.BlockSpec((ts, d), row),
            _layer_spec(gm, layer),
            pl.BlockSpec((ts, 1), row),
            _const_spec(w_in.shape),
            _const_spec(gq.shape),
            _const_spec(gkv.shape),
            _const_spec(wqn.shape),
            _const_spec(wqr.shape),
            _const_spec(wk.shape),
            _const_spec(wv.shape),
            _const_spec(freq.shape),
            _const_spec(sign.shape),
        ],
        out_specs=[
            pl.BlockSpec((ts, qk_width), row),
            pl.BlockSpec((ts, qk_width), row),
            pl.BlockSpec((ts, wv.shape[1]), row),
        ],
        out_shape=[
            jax.ShapeDtypeStruct((n, qk_width), BF16),
            jax.ShapeDtypeStruct((n, qk_width), BF16),
            jax.ShapeDtypeStruct((n, wv.shape[1]), BF16),
        ],
        compiler_params=_params(("arbitrary",)),
        name="mla_proj",
    )(x2d, gm, pos, w_in, gq, gkv, wqn, wqr, wk, wv, freq, sign)


def _attn_body(scale, q_ref, k_ref, v_ref, o_ref, vaug, s_buf, p_buf):
    seq = q_ref.shape[0]
    tq = TQ_ATTN
    nq = seq // tq
    dk = QK_PAD_DIM
    dv = V_HEAD_DIM
    heads = v_ref.shape[1] // dv
    c = scale * math.log2(math.e)
    row = lax.broadcasted_iota(jnp.int32, (tq, tq), 0)
    col = lax.broadcasted_iota(jnp.int32, (tq, tq), 1)
    causal = col <= row
    for h in range(heads):
        vaug[:, 2 * h * dv:(2 * h + 1) * dv] = v_ref[:, h * dv:(h + 1) * dv]
        vaug[:, (2 * h + 1) * dv:(2 * h + 2) * dv] = jnp.ones((seq, dv), BF16)

    tiles = []
    for h in range(heads):
        order = range(nq) if h % 2 == 0 else range(nq - 1, -1, -1)
        tiles += [(h, qi) for qi in order]

    scaled_row_max = {}

    def scores(t):
        h, qi = tiles[t]
        lo, nk, b = qi * tq, (qi + 1) * tq, t % 2
        s = _dot_nt(q_ref[lo:nk, h * dk:(h + 1) * dk], k_ref[0:nk, h * dk:(h + 1) * dk])
        s_diag = jnp.where(causal, s[:, lo:nk], MASK_VALUE)
        s_buf[b, :, lo:nk] = s_diag
        m_lanes = s_diag[:, 0:LANES]
        for j in range(LANES, tq, LANES):
            m_lanes = jnp.maximum(m_lanes, s_diag[:, j:j + LANES])
        if qi > 0:
            s_buf[b, :, 0:lo] = s[:, 0:lo]
            for j in range(0, lo, LANES):
                m_lanes = jnp.maximum(m_lanes, s[:, j:j + LANES])
        scaled_row_max[t] = jnp.max(m_lanes, axis=-1, keepdims=True) * c

    def softmax(t):
        _, qi = tiles[t]
        nk, b = (qi + 1) * tq, t % 2
        mc = scaled_row_max.pop(t)
        for j in range(0, nk, SOFTMAX_CHUNK):
            s = s_buf[b, :, j:j + SOFTMAX_CHUNK]
            p_buf[b, :, j:j + SOFTMAX_CHUNK] = jnp.exp2(s * c - mc).astype(BF16)

    def weighted_values(t):
        h, qi = tiles[t]
        lo, nk, b = qi * tq, (qi + 1) * tq, t % 2
        acc = _dot(p_buf[b, :, 0:nk], vaug[0:nk, 2 * h * dv:(2 * h + 2) * dv])
        o_ref[lo:nk, h * dv:(h + 1) * dv] = (acc[:, 0:dv] / acc[:, dv:2 * dv]).astype(BF16)

    scores(0)
    for t in range(len(tiles)):
        if t + 1 < len(tiles):
            scores(t + 1)
        softmax(t)
        weighted_values(t)


def _attention(q, k, v, batch, heads):
    n = q.shape[0]
    seq = n // batch
    hp = HEADS_PER_ATTN_STEP
    scale = 1.0 / math.sqrt(QK_NOPE_DIM + QK_ROPE_DIM)
    bh = lambda b, h: (b, h)
    return pl.pallas_call(
        functools.partial(_attn_body, scale),
        grid=(batch, heads // hp),
        in_specs=[
            pl.BlockSpec((seq, hp * QK_PAD_DIM), bh),
            pl.BlockSpec((seq, hp * QK_PAD_DIM), bh),
            pl.BlockSpec((seq, hp * V_HEAD_DIM), bh),
        ],
        out_specs=pl.BlockSpec((seq, hp * V_HEAD_DIM), bh),
        out_shape=jax.ShapeDtypeStruct((n, heads * V_HEAD_DIM), BF16),
        scratch_shapes=[
            pltpu.VMEM((seq, 2 * hp * V_HEAD_DIM), BF16),
            pltpu.VMEM((2, TQ_ATTN, seq), F32),
            pltpu.VMEM((2, TQ_ATTN, seq), BF16),
        ],
        compiler_params=_params(("arbitrary", "arbitrary")),
        name="mla_attention",
    )(q, k, v)


def _split_head_columns(w, head_dim, split):
    r = w.shape[0]
    heads = w.shape[1] // head_dim
    w = w.astype(BF16).reshape(r, heads, head_dim)
    a = w[:, :, :split].reshape(r, heads * split)
    b = w[:, :, split:].reshape(r, heads * (head_dim - split))
    return a, b


def _rope_rows():
    half = QK_ROPE_DIM // 2
    inv_freq = 1.0 / (ROPE_THETA ** (jnp.arange(0, QK_ROPE_DIM, 2, dtype=F32) / QK_ROPE_DIM))
    groups = LANES // QK_ROPE_DIM
    freq = jnp.tile(jnp.concatenate([inv_freq, inv_freq]), groups).reshape(1, LANES)
    sign = jnp.tile(jnp.concatenate([-jnp.ones((half,), F32), jnp.ones((half,), F32)]),
                    groups).reshape(1, LANES)
    return freq, sign


def kernel(x, mem, positions, norm_mix_g, norm_xattn_g, norm_mem_g, norm_ffn_g, pool_w, pool_scale, mla_w_in, mla_q_norm_g, mla_w_q_up, mla_kv_norm_g, mla_w_kv_up, mla_w_out, xattn_w_q, xattn_w_kv, xattn_w_o, ffn_w_in, ffn_w_out, final_norm_g):
    batch, seq, d = x.shape
    depth = norm_mix_g.shape[0]
    n_mixers = 2
    row = lambda g: g.reshape(1, -1)
    rows = lambda g: g.reshape(g.shape[0], 1, g.shape[1])

    gm, gx, gf = rows(norm_mix_g), rows(norm_xattn_g), rows(norm_ffn_g)
    wq_x, wo_x = xattn_w_q.astype(BF16), xattn_w_o.astype(BF16)
    w_pool, s_pool = pool_w.astype(BF16), rows(pool_scale)

    k_mem, v_mem = _memkv(mem, norm_mem_g, xattn_w_kv)
    freq, sign = _rope_rows()
    pos = positions.reshape(batch * seq, 1).astype(F32)

    x2d = x.reshape(batch * seq, d)
    for i in range(depth):
        j = i // n_mixers
        xattn_args = (gx, wq_x, k_mem, v_mem, wo_x, gf)
        if i % n_mixers == 0:
            x2d, h_ffn = _pool_xattn(x2d, batch, i, j, gm, w_pool, s_pool, *xattn_args)
        else:
            rq = mla_q_norm_g.shape[1]
            rkv = mla_kv_norm_g.shape[1]
            w_in = mla_w_in[j].astype(BF16)
            w_in = jnp.pad(w_in, ((0, 0), (0, rq + rkv + LANES - w_in.shape[1])))
            wqn, wqr = _split_head_columns(mla_w_q_up[j], QK_NOPE_DIM + QK_ROPE_DIM, QK_NOPE_DIM)
            wk, wv = _split_head_columns(mla_w_kv_up[j], QK_NOPE_DIM + V_HEAD_DIM, QK_NOPE_DIM)
            heads = wv.shape[1] // V_HEAD_DIM
            q, k, v = _mla_proj(x2d, gm, i, pos, w_in, row(mla_q_norm_g[j]), row(mla_kv_norm_g[j]),
                                wqn, wqr, wk, wv, freq, sign)
            o = _attention(q, k, v, batch, heads)
            x2d, h_ffn = _oproj_xattn(x2d, o, batch, i, j, mla_w_out, *xattn_args)
        x2d = _ffn(h_ffn, x2d, i, ffn_w_in, ffn_w_out, row(final_norm_g), i == depth - 1)
    return x2d.reshape(batch, seq, d)
```
